```python
import jax
import jax.numpy as jnp
from jax import lax
import numpy as np


D_MODEL = 2048
BATCH = 1
SEQ = 16384
DEPTH = 2

GRID_W = 64
CTX_LEN = 256
EPS = 1e-6
N_MOD = 6

A_WIDTH = D_MODEL // 2
A_GROUPS = 8
A_GROUP_DIM = A_WIDTH // A_GROUPS
A_ROWS_PER_CHUNK = 2
A_CHUNK = A_ROWS_PER_CHUNK * GRID_W

SSM_INNER = D_MODEL
SSM_HEAD_DIM = 64
SSM_HEADS = SSM_INNER // SSM_HEAD_DIM
SSM_GROUPS = 8
SSM_HEADS_PER_GROUP = SSM_HEADS // SSM_GROUPS
SSM_STATE = 128
SSM_CONV = 3
SSM_CHUNK = 128
SSM_BC = SSM_GROUPS * SSM_STATE
SSM_CONV_DIM = SSM_INNER + 2 * SSM_BC

F_WIDTH = D_MODEL // 2
F_GROUPS = 4
F_GROUP_DIM = F_WIDTH // F_GROUPS

N_BRANCH = 3
IN_SPLITS = (A_WIDTH, A_WIDTH, SSM_INNER, SSM_CONV_DIM, 2 * SSM_HEADS, F_WIDTH, N_BRANCH * D_MODEL)
IN_COLS = 2 * A_WIDTH + SSM_INNER + SSM_CONV_DIM + 2 * SSM_HEADS + F_WIDTH + N_BRANCH * D_MODEL
SSM_COL0 = 2 * A_WIDTH + SSM_INNER
DT_COL0 = SSM_COL0 + SSM_CONV_DIM

MOE_GROUPS = 8
MOE_EXPERTS_PER_GROUP = 8
MOE_EXPERTS = MOE_GROUPS * MOE_EXPERTS_PER_GROUP
MOE_TOPK = 2
MOE_FF = 512
MOE_BLOCK = 128

kernel_name = 'hybrid_sgu_ssd_fnet_hmoe_dit'


def _split(t, sizes):
    idx = np.cumsum(sizes)[:-1].tolist()
    return jnp.split(t, idx, axis=-1)


def _rmsnorm(x, g):
    xf = x.astype(jnp.float32)
    y = xf * lax.rsqrt(jnp.mean(xf * xf, axis=-1, keepdims=True) + EPS)
    return y.astype(x.dtype) * g


def _layernorm(x, g, b):
    xf = x.astype(jnp.float32)
    mu = jnp.mean(xf, axis=-1, keepdims=True)
    var = jnp.mean(jnp.square(xf - mu), axis=-1, keepdims=True)
    return ((xf - mu) * lax.rsqrt(var + EPS)).astype(x.dtype) * g + b


def _modulate(xn, shift, scale):
    return xn * (1 + scale) + shift


def _spatial_gating(u, v, ln_g, ln_b, w_sp, b_sp, n_chunks):
    bsz, L, _ = v.shape
    v = _layernorm(v, ln_g, ln_b).reshape(bsz, n_chunks, A_CHUNK, A_GROUPS, A_GROUP_DIM)
    s = jnp.einsum('gts,bcsgd->bctgd', w_sp, v) + b_sp.T[None, None, :, :, None]
    return u * s.reshape(bsz, L, A_WIDTH)


def _fourier(f):
    bsz, L, _ = f.shape
    fg = f.astype(jnp.float32).reshape(bsz, L, F_GROUPS, F_GROUP_DIM)
    out = jnp.fft.fft2(fg, axes=(1, 3), norm='ortho').real
    return out.reshape(bsz, L, F_WIDTH).astype(f.dtype)


def _dwconv(x, w, b):
    y = lax.conv_general_dilated(x, w[:, None, :], window_strides=(1,),
                                 padding=[(SSM_CONV // 2, SSM_CONV // 2)],
                                 dimension_numbers=('NWC', 'WIO', 'NWC'),
                                 feature_group_count=x.shape[-1])
    return y + b


def _ssd_prep(xs, dt, a, bm):
    bsz, L = xs.shape[:2]
    nc = L // SSM_CHUNK
    R = SSM_HEADS_PER_GROUP
    xsc = xs.astype(jnp.float32).reshape(bsz, nc, SSM_CHUNK, SSM_GROUPS, R, SSM_HEAD_DIM)
    dtc = dt.reshape(bsz, nc, SSM_CHUNK, SSM_GROUPS, R)
    xdt = xsc * dtc[..., None]
    acs = jnp.cumsum(dtc * a.reshape(SSM_GROUPS, R), axis=2)
    bmc = bm.astype(jnp.float32).reshape(bsz, nc, SSM_CHUNK, SSM_GROUPS, SSM_STATE)
    return xdt, acs, bmc


def _ssd_scan_states(xdt, acs, bmc, init):
    decay_end = jnp.exp(acs[:, :, -1:] - acs)
    chunk_states = jnp.einsum('bcsgn,bcsgrp->bcgrpn', bmc, xdt * decay_end[..., None])
    chunk_decay = jnp.exp(acs[:, :, -1])

    def step(hs, inp):
        st, dec = inp
        return dec[..., None, None] * hs + st, hs

    final, enter = lax.scan(step, init, (jnp.moveaxis(chunk_states, 1, 0), jnp.moveaxis(chunk_decay, 1, 0)))
    return jnp.moveaxis(enter, 0, 1), final


def _ssd(xs, dt, a, bm, cm, init):
    bsz, L = xs.shape[:2]
    xdt, acs, bmc = _ssd_prep(xs, dt, a, bm)
    cmc = cm.astype(jnp.float32).reshape(bmc.shape)
    enter, final = _ssd_scan_states(xdt, acs, bmc, init)
    mask = jnp.tril(jnp.ones((SSM_CHUNK, SSM_CHUNK), dtype=bool))[:, :, None, None]
    seg = acs[:, :, :, None] - acs[:, :, None, :]
    decay = jnp.exp(jnp.where(mask, seg, -jnp.inf))
    cb = jnp.einsum('bclgn,bcsgn->bclsg', cmc, bmc)
    y = jnp.einsum('bclsgr,bcsgrp->bclgrp', cb[..., None] * decay, xdt)
    y = y + jnp.einsum('bclgn,bcgrpn->bclgrp', cmc, enter) * jnp.exp(acs)[..., None]
    return y.reshape(bsz, L, SSM_HEADS, SSM_HEAD_DIM), final


def _direction(d):
    if d == 1:
        return lambda t: jnp.flip(t, axis=1)
    return lambda t: t


def _dir_dt_a(rev, dt_raw_d, dt_bias_d, a_log_d):
    dt = jax.nn.softplus(rev(dt_raw_d).astype(jnp.float32) + dt_bias_d.astype(jnp.float32))
    return dt, -jnp.exp(a_log_d.astype(jnp.float32))


def _bidir_ssm(xs, bm, cm, dt_raw, dt_bias, a_log, d_skip, init):
    bsz, L = xs.shape[:2]
    dt_raw = dt_raw.reshape(bsz, L, 2, SSM_HEADS)
    xf = xs.astype(jnp.float32)
    y = jnp.zeros(xs.shape, jnp.float32)
    finals = []
    for d in range(2):
        rev = _direction(d)
        dt, a = _dir_dt_a(rev, dt_raw[:, :, d], dt_bias[d], a_log[d])
        yd, fin = _ssd(rev(xs), dt, a, rev(bm), rev(cm), init[d])
        y = y + rev(yd) + d_skip[d].astype(jnp.float32)[:, None] * xf
        finals.append(fin)
    return y, (finals[0], finals[1])


def _ssm_inputs(xbc_raw, conv_w, conv_b):
    bsz, L, _ = xbc_raw.shape
    xbc = jax.nn.silu(_dwconv(xbc_raw, conv_w, conv_b))
    xs, bm, cm = _split(xbc, (SSM_INNER, SSM_BC, SSM_BC))
    return (xs.reshape(bsz, L, SSM_HEADS, SSM_HEAD_DIM),
            bm.reshape(bsz, L, SSM_GROUPS, SSM_STATE),
            cm.reshape(bsz, L, SSM_GROUPS, SSM_STATE))


def _token_mixer(xn, p, n_chunks, init):
    bsz, L, _ = xn.shape
    proj = xn @ p['w_in']
    ua, va, z, xbc, dt_raw, fc, gl = _split(proj, IN_SPLITS)
    y_a = _spatial_gating(jax.nn.gelu(ua, approximate=False), jax.nn.gelu(va, approximate=False),
                          p['ln_a_g'], p['ln_a_b'], p['w_sp'], p['b_sp'], n_chunks)
    xs, bm, cm = _ssm_inputs(xbc, p['conv_w'], p['conv_b'])
    y_s, finals = _bidir_ssm(xs, bm, cm, dt_raw, p['dt_bias'], p['a_log'], p['d_skip'], init)
    y_b = _rmsnorm(y_s.reshape(bsz, L, SSM_INNER) * jax.nn.silu(z.astype(jnp.float32)), p['g_ssm']).astype(xn.dtype)
    y_c = _fourier(fc)
    gates = jax.nn.sigmoid(gl + p['b_gate']).reshape(bsz, L, N_BRANCH, D_MODEL)
    merged = (gates[:, :, 0] * (y_a @ p['w_pa']) + gates[:, :, 1] * (y_b @ p['w_pb'])
              + gates[:, :, 2] * (y_c @ p['w_pc']))
    return merged @ p['w_out'], finals


def _context_ssm_states(xn, p, init):
    bsz, L, _ = xn.shape
    n_xb = SSM_INNER + SSM_BC
    xb_raw = xn @ p['w_in'][:, SSM_COL0:SSM_COL0 + n_xb]
    dt_raw = (xn @ p['w_in'][:, DT_COL0:DT_COL0 + 2 * SSM_HEADS]).reshape(bsz, L, 2, SSM_HEADS)
    xb = jax.nn.silu(_dwconv(xb_raw, p['conv_w'][:, :n_xb], p['conv_b'][:n_xb]))
    xs, bm = _split(xb, (SSM_INNER, SSM_BC))
    xs = xs.reshape(bsz, L, SSM_HEADS, SSM_HEAD_DIM)
    bm = bm.reshape(bsz, L, SSM_GROUPS, SSM_STATE)
    finals = []
    for d in range(2):
        rev = _direction(d)
        dt, a = _dir_dt_a(rev, dt_raw[:, :, d], p['dt_bias'][d], p['a_log'][d])
        xdt, acs, bmc = _ssd_prep(rev(xs), dt, a, rev(bm))
        _, fin = _ssd_scan_states(xdt, acs, bmc, init[d])
        finals.append(fin)
    return (finals[0], finals[1])


def _hier_moe(xt, w_rg, b_rg, w_re, b_re, w1, w3, w2):
    T, D = xt.shape
    lg = (xt @ w_rg).astype(jnp.float32) + b_rg.astype(jnp.float32)
    pg = jax.nn.softmax(lg, axis=-1)
    _, g_top = lax.top_k(lg, 1)
    p_sel = jnp.take_along_axis(pg, g_top, axis=-1)
    le = ((xt @ w_re).astype(jnp.float32) + b_re.astype(jnp.float32)).reshape(T, MOE_GROUPS, MOE_EXPERTS_PER_GROUP)
    le_sel = jnp.take_along_axis(le, g_top[:, :, None], axis=1)[:, 0]
    tv, ti = lax.top_k(le_sel, MOE_TOPK)
    gate = (p_sel * jax.nn.softmax(tv, axis=-1)).astype(xt.dtype)
    eid = g_top * MOE_EXPERTS_PER_GROUP + ti
    n_asg = T * MOE_TOPK
    flat_e = eid.reshape(n_asg)
    flat_t = jnp.repeat(jnp.arange(T, dtype=jnp.int32), MOE_TOPK)
    flat_w = gate.reshape(n_asg)
    order = jnp.argsort(flat_e)
    se = flat_e[order]
    counts = jnp.bincount(flat_e, length=MOE_EXPERTS)
    padded = (counts + MOE_BLOCK - 1) // MOE_BLOCK * MOE_BLOCK
    pad_end = jnp.cumsum(padded)
    pad_start = pad_end - padded
    start = jnp.cumsum(counts) - counts
    dest = pad_start[se] + jnp.arange(n_asg, dtype=jnp.int32) - start[se]
    n_blocks = (n_asg + MOE_EXPERTS * (MOE_BLOCK - 1) + MOE_BLOCK - 1) // MOE_BLOCK
    n_slots = n_blocks * MOE_BLOCK
    buf_tok = jnp.full((n_slots,), T, jnp.int32).at[dest].set(flat_t[order])
    buf_w = jnp.zeros((n_slots,), xt.dtype).at[dest].set(flat_w[order])
    block_e = jnp.minimum(jnp.searchsorted(pad_end, jnp.arange(n_blocks, dtype=jnp.int32) * MOE_BLOCK, side='right'),
                          MOE_EXPERTS - 1)
    xpad = jnp.concatenate([xt, jnp.zeros((1, D), xt.dtype)], axis=0)

    def expert_block(args):
        tok, e = args
        xb = xpad[tok]
        return (jax.nn.silu(xb @ w1[e]) * (xb @ w3[e])) @ w2[e]

    yb = lax.map(expert_block, (buf_tok.reshape(n_blocks, MOE_BLOCK), block_e))
    y = jnp.zeros((T + 1, D), xt.dtype).at[buf_tok].add(yb.reshape(n_slots, D) * buf_w[:, None])
    return y[:T]


def setup_inputs(seed: int = 0) -> dict:
    key = jax.random.key(seed)
    ks = iter(jax.random.split(key, 48))
    f32 = jnp.float32
    L, D = DEPTH, D_MODEL

    def nrm(shape, scale):
        return scale * jax.random.normal(next(ks), shape, f32)

    dt0 = jnp.exp(jax.random.uniform(next(ks), (L, 2, SSM_HEADS), f32, np.log(1e-3), np.log(1e-1)))
    return {
        'x': nrm((BATCH, SEQ, D), 1.0),
        'c': nrm((BATCH, D), 1.0),
        'ctx': nrm((BATCH, CTX_LEN, D), 1.0),
        'c_ctx': nrm((D,), 1.0),
        'w_ada': nrm((L, D, N_MOD * D), D ** -0.5),
        'b_ada': nrm((L, N_MOD * D), 0.02),
        'g_mix': 1.0 + nrm((L, D), 0.02),
        'w_in': nrm((L, D, IN_COLS), D ** -0.5),
        'b_gate': nrm((L, N_BRANCH * D), 0.02),
        'ln_a_g': 1.0 + nrm((L, A_WIDTH), 0.02),
        'ln_a_b': nrm((L, A_WIDTH), 0.02),
        'w_sp': nrm((L, A_GROUPS, A_CHUNK, A_CHUNK), A_CHUNK ** -0.5),
        'b_sp': 1.0 + nrm((L, A_GROUPS, A_CHUNK), 0.02),
        'conv_w': nrm((L, SSM_CONV, SSM_CONV_DIM), SSM_CONV ** -0.5),
        'conv_b': nrm((L, SSM_CONV_DIM), 0.02),
        'dt_bias': dt0 + jnp.log(-jnp.expm1(-dt0)),
        'a_log': jnp.log(jax.random.uniform(next(ks), (L, 2, SSM_HEADS), f32, 1.0, 16.0)),
        'd_skip': 1.0 + nrm((L, 2, SSM_HEADS), 0.02),
        'g_ssm': 1.0 + nrm((L, SSM_INNER), 0.02),
        'w_pa': nrm((L, A_WIDTH, D), A_WIDTH ** -0.5),
        'w_pb': nrm((L, SSM_INNER, D), SSM_INNER ** -0.5),
        'w_pc': nrm((L, F_WIDTH, D), F_WIDTH ** -0.5),
        'w_out': nrm((L, D, D), D ** -0.5),
        'g_ffn': 1.0 + nrm((L, D), 0.02),
        'w_rg': nrm((L, D, MOE_GROUPS), D ** -0.5),
        'b_rg': nrm((L, MOE_GROUPS), 0.01),
        'w_re': nrm((L, D, MOE_EXPERTS), D ** -0.5),
        'b_re': nrm((L, MOE_EXPERTS), 0.01),
        'w_e1': nrm((L, MOE_EXPERTS, D, MOE_FF), D ** -0.5),
        'w_e3': nrm((L, MOE_EXPERTS, D, MOE_FF), D ** -0.5),
        'w_e2': nrm((L, MOE_EXPERTS, MOE_FF, D), MOE_FF ** -0.5),
        'g_final': 1.0 + nrm((D,), 0.02),
    }


def reference(x, c, ctx, c_ctx, w_ada, b_ada, g_mix, w_in, b_gate, ln_a_g, ln_a_b, w_sp, b_sp,
              conv_w, conv_b, dt_bias, a_log, d_skip, g_ssm, w_pa, w_pb, w_pc, w_out, g_ffn,
              w_rg, b_rg, w_re, b_re, w_e1, w_e3, w_e2, g_final):
    bsz = x.shape[0]
    rows = x.shape[1] // GRID_W
    lat_chunks = rows // A_ROWS_PER_CHUNK
    ctx_chunks = ctx.shape[1] // A_CHUNK
    zero_state = jnp.zeros((bsz, SSM_GROUPS, SSM_HEADS_PER_GROUP, SSM_HEAD_DIM, SSM_STATE), jnp.float32)
    h, hc = x, ctx
    for l in range(DEPTH):
        last = l == DEPTH - 1
        mp = {'w_in': w_in[l], 'b_gate': b_gate[l], 'ln_a_g': ln_a_g[l], 'ln_a_b': ln_a_b[l],
              'w_sp': w_sp[l], 'b_sp': b_sp[l], 'conv_w': conv_w[l], 'conv_b': conv_b[l],
              'dt_bias': dt_bias[l], 'a_log': a_log[l], 'd_skip': d_skip[l], 'g_ssm': g_ssm[l],
              'w_pa': w_pa[l], 'w_pb': w_pb[l], 'w_pc': w_pc[l], 'w_out': w_out[l]}
        mod_l = jnp.split((jax.nn.silu(c) @ w_ada[l] + b_ada[l])[:, None, :], N_MOD, axis=-1)
        mod_c = jnp.split(jax.nn.silu(c_ctx) @ w_ada[l] + b_ada[l], N_MOD, axis=-1)
        hn = _modulate(_rmsnorm(h, g_mix[l]), mod_l[0], mod_l[1])
        hn_c = _modulate(_rmsnorm(hc, g_mix[l]), mod_c[0], mod_c[1])
        if last:
            ctx_states = _context_ssm_states(hn_c, mp, (zero_state, zero_state))
        else:
            y_c, ctx_states = _token_mixer(hn_c, mp, ctx_chunks, (zero_state, zero_state))
            hc = hc + mod_c[2] * y_c
        y, _ = _token_mixer(hn, mp, lat_chunks, ctx_states)
        h = h + mod_l[2] * y
        moe_args = (w_rg[l], b_rg[l], w_re[l], b_re[l], w_e1[l], w_e3[l], w_e2[l])
        hn = _modulate(_rmsnorm(h, g_ffn[l]), mod_l[3], mod_l[4])
        if last:
            h = h + mod_l[5] * _hier_moe(hn.reshape(-1, D_MODEL), *moe_args).reshape(h.shape)
        else:
            hn_c = _modulate(_rmsnorm(hc, g_ffn[l]), mod_c[3], mod_c[4])
            n_lat = hn.shape[0] * hn.shape[1]
            out = _hier_moe(jnp.concatenate([hn.reshape(-1, D_MODEL), hn_c.reshape(-1, D_MODEL)], axis=0), *moe_args)
            h = h + mod_l[5] * out[:n_lat].reshape(h.shape)
            hc = hc + mod_c[5] * out[n_lat:].reshape(hc.shape)
    return _rmsnorm(h, g_final)
```

```python
import functools

import numpy as np
import jax
import jax.numpy as jnp
from jax import lax
from jax.experimental import pallas as pl
from jax.experimental.pallas import tpu as pltpu

F32 = jnp.float32
BF16 = jnp.bfloat16
HIGHEST = lax.Precision.HIGHEST

EPS = 1e-6
N_MOD = 6
GRID_W = 64
CHUNK = 128
A_GROUPS = 8
SSM_HEADS = 32
SSM_HEAD_DIM = 64
SSM_GROUPS = 8
SSM_STATE = 128
F_GROUPS = 4
N_BRANCH = 3
MOE_GROUPS = 8
MOE_EPG = 8
MOE_EXPERTS = 64
MOE_TOPK = 2
MOE_BLOCK = 128
ROUTER_COLS = 128

V7X_VMEM_LIMIT_BYTES = 56 * 1024 * 1024
SUBLANES = 8


def _params(n_axes):
    return pltpu.CompilerParams(dimension_semantics=("arbitrary",) * n_axes,
                                vmem_limit_bytes=V7X_VMEM_LIMIT_BYTES)


def _pick_tile(n, candidates):
    for t in candidates:
        if n % t == 0:
            return t
    raise ValueError(f"no tile for {n}")


def _gelu(x):
    return 0.5 * x * (1.0 + lax.erf(x * np.float32(1.0 / np.sqrt(2.0))))


def _silu(x):
    return x * jax.nn.sigmoid(x)


def _softplus(x):
    return jnp.maximum(x, 0.0) + jnp.log1p(jnp.exp(-jnp.abs(x)))


def _row_is_ctx(tile_index, tm, t_ctx):
    rows = tile_index * tm + lax.broadcasted_iota(jnp.int32, (tm, 1), 0)
    return rows < t_ctx


def _mod_rows(mod_ref, k, d_model, is_ctx):
    lo = k * d_model
    lat = mod_ref[0:1, lo:lo + d_model]
    ctx = mod_ref[1:2, lo:lo + d_model]
    return jnp.where(is_ctx, ctx, lat)


def _ada_kernel(cs_ref, w_ref, b_ref, o_ref):
    s = _silu(cs_ref[...])
    o_ref[...] = jnp.dot(s, w_ref[...], precision=HIGHEST, preferred_element_type=F32) + b_ref[...]


def _ada(cs, w_ada, b_ada):
    depth, d_model, n = w_ada.shape
    tn = 1024
    return pl.pallas_call(
        _ada_kernel,
        grid=(depth, n // tn),
        in_specs=[pl.BlockSpec((SUBLANES, d_model), lambda l, j: (0, 0)),
                  pl.BlockSpec((None, d_model, tn), lambda l, j: (l, 0, j)),
                  pl.BlockSpec((None, 1, tn), lambda l, j: (l, 0, j))],
        out_specs=pl.BlockSpec((None, SUBLANES, tn), lambda l, j: (l, 0, j)),
        out_shape=jax.ShapeDtypeStruct((depth, SUBLANES, n), F32),
        compiler_params=_params(2),
        name="ada",
    )(cs, w_ada, b_ada.reshape(depth, 1, n))


def _norm_mod_kernel(h_ref, g_ref, mod_ref, o_ref, *, k_shift, k_scale, t_ctx):
    tm, d_model = h_ref.shape
    x = h_ref[...]
    y = x * lax.rsqrt(jnp.mean(x * x, axis=-1, keepdims=True) + EPS) * g_ref[...]
    is_ctx = _row_is_ctx(pl.program_id(0), tm, t_ctx)
    shift = _mod_rows(mod_ref, k_shift, d_model, is_ctx)
    scale = _mod_rows(mod_ref, k_scale, d_model, is_ctx)
    o_ref[...] = (y * (1.0 + scale) + shift).astype(o_ref.dtype)


def _norm_mod(h, g, mod, layer, k_shift, k_scale, t_ctx, out_dtype):
    t_all, d_model = h.shape
    tm = 256
    return pl.pallas_call(
        functools.partial(_norm_mod_kernel, k_shift=k_shift, k_scale=k_scale, t_ctx=t_ctx),
        grid=(t_all // tm,),
        in_specs=[pl.BlockSpec((tm, d_model), lambda i: (i, 0)),
                  pl.BlockSpec((1, d_model), lambda i: (0, 0)),
                  pl.BlockSpec((None, SUBLANES, N_MOD * d_model), lambda i: (layer, 0, 0))],
        out_specs=pl.BlockSpec((tm, d_model), lambda i: (i, 0)),
        out_shape=jax.ShapeDtypeStruct((t_all, d_model), out_dtype),
        compiler_params=_params(1),
        name="norm_mod",
    )(h, g.reshape(1, d_model), mod)


def _mm_kernel(a_ref, w_ref, o_ref):
    o_ref[...] = jnp.dot(a_ref[...], w_ref[...], preferred_element_type=F32).astype(o_ref.dtype)


def _mm(a, w, out_dtype, tn, name):
    m, k = a.shape
    n = w.shape[1]
    tm = _pick_tile(m, (1280, 1024, 640, 512, 256, 128))
    return pl.pallas_call(
        _mm_kernel,
        grid=(n // tn, m // tm),
        in_specs=[pl.BlockSpec((tm, k), lambda j, i: (i, 0)),
                  pl.BlockSpec((k, tn), lambda j, i: (0, j))],
        out_specs=pl.BlockSpec((tm, tn), lambda j, i: (i, j)),
        out_shape=jax.ShapeDtypeStruct((m, n), out_dtype),
        compiler_params=_params(2),
        name=name,
    )(a, w)


def _sgu_kernel(u_ref, v_ref, g_ref, b_ref, wsp_ref, bsp_ref, o_ref):
    v = _gelu(v_ref[...])
    mu = jnp.mean(v, axis=-1, keepdims=True)
    var = jnp.mean(jnp.square(v - mu), axis=-1, keepdims=True)
    vn = ((v - mu) * lax.rsqrt(var + EPS)) * g_ref[...] + b_ref[...]
    vb = vn.astype(BF16)
    gd = v.shape[-1] // A_GROUPS
    for g in range(A_GROUPS):
        cols = slice(g * gd, (g + 1) * gd)
        s = jnp.dot(wsp_ref[g], vb[:, cols], preferred_element_type=F32) + bsp_ref[:, g:g + 1]
        o_ref[:, cols] = (_gelu(u_ref[:, cols]) * s).astype(o_ref.dtype)


def _sgu(proj, ln_g, ln_b, w_sp, b_sp, a_width):
    t_all = proj.shape[0]
    return pl.pallas_call(
        _sgu_kernel,
        grid=(t_all // CHUNK,),
        in_specs=[pl.BlockSpec((CHUNK, a_width), lambda c: (c, 0)),
                  pl.BlockSpec((CHUNK, a_width), lambda c: (c, 1)),
                  pl.BlockSpec((1, a_width), lambda c: (0, 0)),
                  pl.BlockSpec((1, a_width), lambda c: (0, 0)),
                  pl.BlockSpec((A_GROUPS, CHUNK, CHUNK), lambda c: (0, 0, 0)),
                  pl.BlockSpec((CHUNK, A_GROUPS), lambda c: (0, 0))],
        out_specs=pl.BlockSpec((CHUNK, a_width), lambda c: (c, 0)),
        out_shape=jax.ShapeDtypeStruct((t_all, a_width), BF16),
        compiler_params=_params(1),
        name="sgu",
    )(proj, proj, ln_g.reshape(1, a_width), ln_b.reshape(1, a_width), w_sp.astype(BF16), b_sp.T)


def _ssd_chunk_of(i, direction, n_ctx_chunks, n_chunks):
    if direction == 0:
        return i
    return jnp.where(i < n_ctx_chunks, n_ctx_chunks - 1 - i, n_chunks - 1 - (i - n_ctx_chunks))


def _ssd_kernel(*refs, direction, n_ctx_chunks, n_chunks, final):
    if final:
        (xbc_ref, prev_ref, next_ref, dt_ref, cw_ref, cb_ref, dtb_ref, alog_ref, dsk_ref,
         yprev_ref, z_ref, gs_ref, o_ref, state_ref, t_ref) = refs
    else:
        (xbc_ref, prev_ref, next_ref, dt_ref, cw_ref, cb_ref, dtb_ref, alog_ref, dsk_ref,
         o_ref, state_ref) = refs
    L = CHUNK
    i = pl.program_id(0)
    c = _ssd_chunk_of(i, direction, n_ctx_chunks, n_chunks)
    is_first = jnp.logical_or(c == 0, c == n_ctx_chunks)
    is_last = jnp.logical_or(c == n_ctx_chunks - 1, c == n_chunks - 1)

    @pl.when(i == 0)
    def _():
        state_ref[...] = jnp.zeros_like(state_ref)

    row = lax.broadcasted_iota(jnp.int32, (L, L), 0)
    col = lax.broadcasted_iota(jnp.int32, (L, L), 1)
    causal = (col <= row) if direction == 0 else (col >= row)
    row1 = lax.broadcasted_iota(jnp.int32, (L, 1), 0)

    dt = _softplus(dt_ref[...] + dtb_ref[...])
    a = -jnp.exp(alog_ref[...])
    acs = jnp.dot(causal.astype(F32), dt * a, precision=HIGHEST, preferred_element_type=F32)
    acs_end = acs[L - 1:L, :] if direction == 0 else acs[0:1, :]
    m_all = dt * jnp.exp(acs_end - acs)
    e_in = jnp.exp(acs)
    chunk_decay = jnp.exp(acs_end)
    acs_t = acs.T
    dt_t = dt.T

    def conv_silu(c0, w):
        x = xbc_ref[:, c0:c0 + w]
        pr = jnp.where(is_first, 0.0, prev_ref[SUBLANES - 1:SUBLANES, c0:c0 + w])
        nx = jnp.where(is_last, 0.0, next_ref[0:1, c0:c0 + w])
        xm = jnp.where(row1 == 0, pr, pltpu.roll(x, 1, 0))
        xp = jnp.where(row1 == L - 1, nx, pltpu.roll(x, L - 1, 0))
        y = (xm * cw_ref[0:1, c0:c0 + w] + x * cw_ref[1:2, c0:c0 + w]
             + xp * cw_ref[2:3, c0:c0 + w] + cb_ref[:, c0:c0 + w])
        return _silu(y)

    inner = SSM_HEADS * SSM_HEAD_DIM
    bc = SSM_GROUPS * SSM_STATE
    pair_w = 2 * SSM_HEAD_DIM
    lane = lax.broadcasted_iota(jnp.int32, (L, pair_w), 1)
    lo_half = lane < SSM_HEAD_DIM
    prow = lax.broadcasted_iota(jnp.int32, (pair_w, 1), 0)
    lo_rows = prow < SSM_HEAD_DIM
    ssq = jnp.zeros((L, 1), F32)

    for g in range(SSM_GROUPS):
        b_g = conv_silu(inner + g * SSM_STATE, SSM_STATE)
        c_g = conv_silu(inner + bc + g * SSM_STATE, SSM_STATE)
        cb = lax.dot_general(c_g.astype(BF16), b_g.astype(BF16), (((1,), (1,)), ((), ())),
                             preferred_element_type=F32)
        for jj in range(2):
            j = 2 * g + jj
            cols = [direction * SSM_HEADS + 2 * j, direction * SSM_HEADS + 2 * j + 1]
            x2f = conv_silu(j * pair_w, pair_w)
            x2 = x2f.astype(BF16)
            zero = jnp.zeros_like(x2)
            xd = jnp.concatenate([jnp.where(lo_half, x2, zero), jnp.where(lo_half, zero, x2)], axis=0)
            l_parts, c_parts, b_parts = [], [], []
            for cc in cols:
                seg = acs[:, cc:cc + 1] - acs_t[cc:cc + 1, :]
                dec = jnp.exp(jnp.where(causal, seg, -jnp.inf))
                l_parts.append((cb * dec * dt_t[cc:cc + 1, :]).astype(BF16))
                c_parts.append((c_g * e_in[:, cc:cc + 1]).astype(BF16))
                b_parts.append((b_g * m_all[:, cc:cc + 1]).astype(BF16))
            l2 = jnp.concatenate(l_parts, axis=1)
            c2 = jnp.concatenate(c_parts, axis=1)
            b2 = jnp.concatenate(b_parts, axis=0)
            sp = state_ref[j]
            spb = sp.astype(BF16)
            zs = jnp.zeros_like(spb)
            sd = jnp.concatenate([jnp.where(lo_rows, spb, zs), jnp.where(lo_rows, zs, spb)], axis=1)
            y = jnp.dot(l2, xd, preferred_element_type=F32)
            y = y + lax.dot_general(c2, sd, (((1,), (1,)), ((), ())), preferred_element_type=F32)
            y = y + dsk_ref[:, j * pair_w:(j + 1) * pair_w] * x2f
            upd = lax.dot_general(xd, b2, (((0,), (0,)), ((), ())), preferred_element_type=F32)
            cd = jnp.where(lo_rows, chunk_decay[:, cols[0]:cols[0] + 1], chunk_decay[:, cols[1]:cols[1] + 1])
            state_ref[j] = cd * sp + upd
            ysl = slice(j * pair_w, (j + 1) * pair_w)
            if final:
                t = (yprev_ref[:, ysl] + y) * _silu(z_ref[:, ysl])
                ssq = ssq + jnp.sum(t * t, axis=-1, keepdims=True)
                t_ref[:, ysl] = t
            else:
                o_ref[:, ysl] = y
    if final:
        scale = lax.rsqrt(ssq * np.float32(1.0 / inner) + EPS)
        o_ref[...] = (t_ref[...] * scale * gs_ref[...]).astype(o_ref.dtype)


def _ssd(proj, xbc_blk0, dt_raw, conv_w, conv_b, dt_bias_p, a_log_p, d_skip_x, direction,
         n_ctx_chunks, yprev=None, z_blk0=None, g_ssm=None):
    t_all = proj.shape[0]
    n_chunks = t_all // CHUNK
    inner = SSM_HEADS * SSM_HEAD_DIM
    conv_dim = conv_w.shape[1]
    final = yprev is not None
    chunk_of = functools.partial(_ssd_chunk_of, direction=direction, n_ctx_chunks=n_ctx_chunks,
                                 n_chunks=n_chunks)
    rows8 = CHUNK // SUBLANES
    last8 = t_all // SUBLANES - 1
    in_specs = [
        pl.BlockSpec((CHUNK, conv_dim), lambda i: (chunk_of(i), xbc_blk0)),
        pl.BlockSpec((SUBLANES, conv_dim), lambda i: (jnp.maximum(chunk_of(i) * rows8 - 1, 0), xbc_blk0)),
        pl.BlockSpec((SUBLANES, conv_dim), lambda i: (jnp.minimum(chunk_of(i) * rows8 + rows8, last8), xbc_blk0)),
        pl.BlockSpec((CHUNK, 128), lambda i: (chunk_of(i), 0)),
        pl.BlockSpec((3, conv_dim), lambda i: (0, 0)),
        pl.BlockSpec((1, conv_dim), lambda i: (0, 0)),
        pl.BlockSpec((1, 128), lambda i: (0, 0)),
        pl.BlockSpec((1, 128), lambda i: (0, 0)),
        pl.BlockSpec((1, inner), lambda i: (0, 0)),
    ]
    args = [proj, proj, proj, dt_raw, conv_w, conv_b.reshape(1, conv_dim), dt_bias_p, a_log_p, d_skip_x]
    scratch = [pltpu.VMEM((SSM_HEADS // 2, 2 * SSM_HEAD_DIM, SSM_STATE), F32)]
    if final:
        in_specs += [pl.BlockSpec((CHUNK, inner), lambda i: (chunk_of(i), 0)),
                     pl.BlockSpec((CHUNK, inner), lambda i: (chunk_of(i), z_blk0)),
                     pl.BlockSpec((1, inner), lambda i: (0, 0))]
        args += [yprev, proj, g_ssm.reshape(1, inner)]
        scratch.append(pltpu.VMEM((CHUNK, inner), F32))
    return pl.pallas_call(
        functools.partial(_ssd_kernel, direction=direction, n_ctx_chunks=n_ctx_chunks,
                          n_chunks=n_chunks, final=final),
        grid=(n_chunks,),
        in_specs=in_specs,
        out_specs=pl.BlockSpec((CHUNK, inner), lambda i: (chunk_of(i), 0)),
        out_shape=jax.ShapeDtypeStruct((t_all, inner), BF16 if final else F32),
        scratch_shapes=scratch,
        compiler_params=_params(1),
        name="ssd_bwd" if final else "ssd_fwd",
    )(*args)


def _dft_tables(n):
    k = np.arange(n, dtype=np.int64)
    ang = 2.0 * np.pi * ((k[:, None] * k[None, :]) % n).astype(np.float64) / n
    return np.cos(ang), np.sin(ang)


def _chan_dft_kernel(a_ref, w_ref, o_ref):
    o_ref[...] = jnp.dot(a_ref[...], w_ref[...], preferred_element_type=F32).astype(o_ref.dtype)


def _chan_dft(f, gd):
    rows, width = f.shape
    groups = width // gd
    cc, sc = _dft_tables(gd)
    w = jnp.asarray(np.concatenate([cc, -sc], axis=1), BF16)
    tm = _pick_tile(rows, (1024, 512, 256, 128))
    return pl.pallas_call(
        _chan_dft_kernel,
        grid=(rows // tm, groups),
        in_specs=[pl.BlockSpec((tm, gd), lambda i, g: (i, g)),
                  pl.BlockSpec((gd, 2 * gd), lambda i, g: (0, 0))],
        out_specs=pl.BlockSpec((tm, 2 * gd), lambda i, g: (i, g)),
        out_shape=jax.ShapeDtypeStruct((rows, 2 * width), BF16),
        compiler_params=_params(2),
        name="fnet_chan",
    )(f, w)


def _pos_dft1_kernel(y_ref, w_ref, tc_ref, ts_ref, o_ref, *, groups, gd):
    n1 = y_ref.shape[0]
    p = jnp.dot(w_ref[...], y_ref[...], preferred_element_type=F32)
    tc = tc_ref[...]
    ts = ts_ref[...]
    for g in range(groups):
        re = slice(g * 2 * gd, g * 2 * gd + gd)
        im = slice(g * 2 * gd + gd, (g + 1) * 2 * gd)
        zr = p[0:n1, re] - p[n1:2 * n1, im]
        zi = p[0:n1, im] + p[n1:2 * n1, re]
        o_ref[:, re] = (zr * tc + zi * ts).astype(o_ref.dtype)
        o_ref[:, im] = (zi * tc - zr * ts).astype(o_ref.dtype)


def _pos_dft2_kernel(z_ref, w_ref, o_ref, *, groups, gd, scale):
    for g in range(groups):
        re = slice(g * 2 * gd, g * 2 * gd + gd)
        im = slice(g * 2 * gd + gd, (g + 1) * 2 * gd)
        zz = jnp.concatenate([z_ref[:, re], z_ref[:, im]], axis=0)
        out = jnp.dot(w_ref[...], zz, preferred_element_type=F32)
        o_ref[:, g * gd:(g + 1) * gd] = (out * scale).astype(o_ref.dtype)


def _pos_dft2(z, n2, groups, gd, scale):
    rows = z.shape[0]
    c2, s2 = _dft_tables(n2)
    w2 = jnp.asarray(np.concatenate([c2, s2], axis=1), BF16)
    return pl.pallas_call(
        functools.partial(_pos_dft2_kernel, groups=groups, gd=gd, scale=np.float32(scale)),
        grid=(rows // n2,),
        in_specs=[pl.BlockSpec((n2, groups * 2 * gd), lambda i: (i, 0)),
                  pl.BlockSpec((n2, 2 * n2), lambda i: (0, 0))],
        out_specs=pl.BlockSpec((n2, groups * gd), lambda i: (i, 0)),
        out_shape=jax.ShapeDtypeStruct((rows, groups * gd), BF16),
        compiler_params=_params(1),
        name="fnet_pos2",
    )(z, w2)


def _fourier_long(f, gd):
    length, width = f.shape
    groups = width // gd
    n2 = CHUNK
    n1 = length // n2
    cw = groups * 2 * gd
    y = _chan_dft(f, gd)
    yt = y.reshape(n1, n2, cw).transpose(1, 0, 2).reshape(length, cw)
    c1, s1 = _dft_tables(n1)
    w1 = jnp.asarray(np.concatenate([c1, -s1], axis=0), BF16)
    ang = 2.0 * np.pi * (np.arange(n2)[:, None] * np.arange(n1)[None, :]).astype(np.float64) / length
    tc = jnp.asarray(np.cos(ang)[:, :, None], F32)
    ts = jnp.asarray(np.sin(ang)[:, :, None], F32)
    zt = pl.pallas_call(
        functools.partial(_pos_dft1_kernel, groups=groups, gd=gd),
        grid=(n2,),
        in_specs=[pl.BlockSpec((n1, cw), lambda i: (i, 0)),
                  pl.BlockSpec((2 * n1, n1), lambda i: (0, 0)),
                  pl.BlockSpec((None, n1, 1), lambda i: (i, 0, 0)),
                  pl.BlockSpec((None, n1, 1), lambda i: (i, 0, 0))],
        out_specs=pl.BlockSpec((n1, cw), lambda i: (i, 0)),
        out_shape=jax.ShapeDtypeStruct((length, cw), BF16),
        compiler_params=_params(1),
        name="fnet_pos1",
    )(yt, w1, tc, ts)
    z = zt.reshape(n2, n1, cw).transpose(1, 0, 2).reshape(length, cw)
    o = _pos_dft2(z, n2, groups, gd, 1.0 / np.sqrt(float(length) * gd))
    return o.reshape(n1, n2, width).transpose(1, 0, 2).reshape(length, width)


def _fourier_short(f, gd):
    length, width = f.shape
    groups = width // gd
    y = _chan_dft(f, gd)
    return _pos_dft2(y, length, groups, gd, 1.0 / np.sqrt(float(length) * gd))


def _merge_kernel(ya_ref, yb_ref, yc_ref, g0_ref, g1_ref, g2_ref, bg_ref, wa_ref, wb_ref, wc_ref, o_ref):
    tn = o_ref.shape[1]
    acc = jax.nn.sigmoid(g0_ref[...] + bg_ref[0:1, :]) * jnp.dot(ya_ref[...], wa_ref[...], preferred_element_type=F32)
    acc = acc + jax.nn.sigmoid(g1_ref[...] + bg_ref[1:2, :]) * jnp.dot(yb_ref[...], wb_ref[...], preferred_element_type=F32)
    acc = acc + jax.nn.sigmoid(g2_ref[...] + bg_ref[2:3, :]) * jnp.dot(yc_ref[...], wc_ref[...], preferred_element_type=F32)
    o_ref[...] = acc.astype(o_ref.dtype)


def _merge(ya, yb, yc, proj, gl_col0, b_gate, w_pa, w_pb, w_pc):
    t_all = ya.shape[0]
    d_model = w_pa.shape[1]
    tn = 512
    tm = _pick_tile(t_all, (640, 512, 256, 128))
    gblk0 = gl_col0 // tn
    nblk = d_model // tn
    return pl.pallas_call(
        _merge_kernel,
        grid=(d_model // tn, t_all // tm),
        in_specs=[pl.BlockSpec((tm, ya.shape[1]), lambda j, i: (i, 0)),
                  pl.BlockSpec((tm, yb.shape[1]), lambda j, i: (i, 0)),
                  pl.BlockSpec((tm, yc.shape[1]), lambda j, i: (i, 0)),
                  pl.BlockSpec((tm, tn), lambda j, i: (i, gblk0 + j)),
                  pl.BlockSpec((tm, tn), lambda j, i: (i, gblk0 + nblk + j)),
                  pl.BlockSpec((tm, tn), lambda j, i: (i, gblk0 + 2 * nblk + j)),
                  pl.BlockSpec((N_BRANCH, tn), lambda j, i: (0, j)),
                  pl.BlockSpec((w_pa.shape[0], tn), lambda j, i: (0, j)),
                  pl.BlockSpec((w_pb.shape[0], tn), lambda j, i: (0, j)),
                  pl.BlockSpec((w_pc.shape[0], tn), lambda j, i: (0, j))],
        out_specs=pl.BlockSpec((tm, tn), lambda j, i: (i, j)),
        out_shape=jax.ShapeDtypeStruct((t_all, d_model), BF16),
        compiler_params=_params(2),
        name="merge",
    )(ya, yb, yc, proj, proj, proj, b_gate.reshape(N_BRANCH, d_model),
      w_pa.astype(BF16), w_pb.astype(BF16), w_pc.astype(BF16))


def _proj_res_kernel(a_ref, w_ref, h_ref, mod_ref, o_ref, *, k_gate, t_ctx, d_model):
    tm, tn = o_ref.shape
    j = pl.program_id(0)
    is_ctx = _row_is_ctx(pl.program_id(1), tm, t_ctx)
    acc = jnp.dot(a_ref[...], w_ref[...], preferred_element_type=F32)
    gate = jnp.where(is_ctx, mod_ref[1:2, :], mod_ref[0:1, :])
    o_ref[...] = h_ref[...] + gate * acc


def _proj_res(a, w, h, mod, layer, k_gate, t_ctx):
    t_all, d_model = h.shape
    tn = 512
    tm = _pick_tile(t_all, (640, 512, 256, 128))
    nblk = d_model // tn
    return pl.pallas_call(
        functools.partial(_proj_res_kernel, k_gate=k_gate, t_ctx=t_ctx, d_model=d_model),
        grid=(d_model // tn, t_all // tm),
        in_specs=[pl.BlockSpec((tm, a.shape[1]), lambda j, i: (i, 0)),
                  pl.BlockSpec((a.shape[1], tn), lambda j, i: (0, j)),
                  pl.BlockSpec((tm, tn), lambda j, i: (i, j)),
                  pl.BlockSpec((None, SUBLANES, tn), lambda j, i: (layer, 0, k_gate * nblk + j))],
        out_specs=pl.BlockSpec((tm, tn), lambda j, i: (i, j)),
        out_shape=jax.ShapeDtypeStruct((t_all, d_model), F32),
        compiler_params=_params(2),
        name="proj_res",
    )(a, w.astype(BF16), h, mod)


def _ffn_norm_kernel(h_ref, g_ref, mod_ref, wr_ref, br_ref, x_ref, lg_ref, *, t_ctx):
    tm, d_model = h_ref.shape
    x = h_ref[...]
    y = x * lax.rsqrt(jnp.mean(x * x, axis=-1, keepdims=True) + EPS) * g_ref[...]
    is_ctx = _row_is_ctx(pl.program_id(0), tm, t_ctx)
    shift = _mod_rows(mod_ref, 3, d_model, is_ctx)
    scale = _mod_rows(mod_ref, 4, d_model, is_ctx)
    xn = y * (1.0 + scale) + shift
    x_ref[...] = xn.astype(x_ref.dtype)
    lg_ref[...] = jnp.dot(xn, wr_ref[...], precision=HIGHEST, preferred_element_type=F32) + br_ref[...]


def _ffn_norm(h, g, mod, layer, w_router, b_router, t_ctx):
    t_all, d_model = h.shape
    tm = 256
    return pl.pallas_call(
        functools.partial(_ffn_norm_kernel, t_ctx=t_ctx),
        grid=(t_all // tm,),
        in_specs=[pl.BlockSpec((tm, d_model), lambda i: (i, 0)),
                  pl.BlockSpec((1, d_model), lambda i: (0, 0)),
                  pl.BlockSpec((None, SUBLANES, N_MOD * d_model), lambda i: (layer, 0, 0)),
                  pl.BlockSpec((d_model, ROUTER_COLS), lambda i: (0, 0)),
                  pl.BlockSpec((1, ROUTER_COLS), lambda i: (0, 0))],
        out_specs=[pl.BlockSpec((tm, d_model), lambda i: (i, 0)),
                   pl.BlockSpec((tm, ROUTER_COLS), lambda i: (i, 0))],
        out_shape=[jax.ShapeDtypeStruct((t_all, d_model), BF16),
                   jax.ShapeDtypeStruct((t_all, ROUTER_COLS), F32)],
        compiler_params=_params(1),
        name="ffn_norm",
    )(h, g.reshape(1, d_model), mod, w_router, b_router)


def _moe_kernel(be_ref, nact_ref, x_ref, w1_ref, w3_ref, w2_ref, bw_ref, o_ref, w1b, w3b, w2b):
    b = pl.program_id(0)
    active = b < nact_ref[0]
    prev_e = be_ref[jnp.maximum(b - 1, 0)]
    new_expert = jnp.logical_or(b == 0, be_ref[b] != prev_e)

    @pl.when(jnp.logical_and(active, new_expert))
    def _():
        w1b[...] = w1_ref[...].astype(BF16)
        w3b[...] = w3_ref[...].astype(BF16)
        w2b[...] = w2_ref[...].astype(BF16)

    @pl.when(active)
    def _():
        x = x_ref[...]
        h1 = jnp.dot(x, w1b[...], preferred_element_type=F32)
        h3 = jnp.dot(x, w3b[...], preferred_element_type=F32)
        hh = (_silu(h1) * h3).astype(BF16)
        o_ref[...] = jnp.dot(hh, w2b[...], preferred_element_type=F32) * bw_ref[...]

    @pl.when(jnp.logical_not(active))
    def _():
        o_ref[...] = jnp.zeros_like(o_ref)


def _moe_blocks(block_e, n_active, x_sorted, w1, w3, w2, buf_w):
    n_slots, d_model = x_sorted.shape
    ff = w1.shape[2]
    n_blocks = n_slots // MOE_BLOCK

    def row_blk(b, be, na):
        return (jnp.minimum(b, na[0] - 1), 0)

    grid_spec = pltpu.PrefetchScalarGridSpec(
        num_scalar_prefetch=2,
        grid=(n_blocks,),
        in_specs=[pl.BlockSpec((MOE_BLOCK, d_model), row_blk),
                  pl.BlockSpec((None, d_model, ff), lambda b, be, na: (be[b], 0, 0)),
                  pl.BlockSpec((None, d_model, ff), lambda b, be, na: (be[b], 0, 0)),
                  pl.BlockSpec((None, ff, d_model), lambda b, be, na: (be[b], 0, 0)),
                  pl.BlockSpec((MOE_BLOCK, 1), row_blk)],
        out_specs=pl.BlockSpec((MOE_BLOCK, d_model), lambda b, be, na: (b, 0)),
        scratch_shapes=[pltpu.VMEM((d_model, ff), BF16), pltpu.VMEM((d_model, ff), BF16),
                        pltpu.VMEM((ff, d_model), BF16)],
    )
    return pl.pallas_call(
        _moe_kernel,
        grid_spec=grid_spec,
        out_shape=jax.ShapeDtypeStruct((n_slots, d_model), F32),
        compiler_params=_params(1),
        name="moe_blocks",
    )(block_e, n_active, x_sorted, w1, w3, w2, buf_w.reshape(n_slots, 1))


def _route(logits, b_unused=None):
    t = logits.shape[0]
    lg = logits[:, :MOE_GROUPS]
    le = logits[:, MOE_GROUPS:MOE_GROUPS + MOE_EXPERTS].reshape(t, MOE_GROUPS, MOE_EPG)
    pg = jax.nn.softmax(lg, axis=-1)
    _, g_top = lax.top_k(lg, 1)
    p_sel = jnp.take_along_axis(pg, g_top, axis=-1)
    le_sel = jnp.take_along_axis(le, g_top[:, :, None], axis=1)[:, 0]
    tv, ti = lax.top_k(le_sel, MOE_TOPK)
    gate = p_sel * jax.nn.softmax(tv, axis=-1)
    eid = g_top * MOE_EPG + ti
    n_asg = t * MOE_TOPK
    flat_e = eid.reshape(n_asg).astype(jnp.int32)
    flat_t = jnp.repeat(jnp.arange(t, dtype=jnp.int32), MOE_TOPK)
    flat_w = gate.reshape(n_asg)
    order = jnp.argsort(flat_e)
    se = flat_e[order]
    counts = jnp.bincount(flat_e, length=MOE_EXPERTS).astype(jnp.int32)
    padded = (counts + MOE_BLOCK - 1) // MOE_BLOCK * MOE_BLOCK
    pad_end = jnp.cumsum(padded)
    pad_start = pad_end - padded
    start = jnp.cumsum(counts) - counts
    dest = pad_start[se] + jnp.arange(n_asg, dtype=jnp.int32) - start[se]
    n_blocks = (n_asg + MOE_EXPERTS * (MOE_BLOCK - 1)) // MOE_BLOCK + 1
    n_slots = n_blocks * MOE_BLOCK
    buf_tok = jnp.zeros((n_slots,), jnp.int32).at[dest].set(flat_t[order])
    buf_w = jnp.zeros((n_slots,), F32).at[dest].set(flat_w[order])
    n_active = (pad_end[-1] // MOE_BLOCK).astype(jnp.int32)
    blk = jnp.arange(n_blocks, dtype=jnp.int32)
    block_e = jnp.minimum(jnp.searchsorted(pad_end, blk * MOE_BLOCK, side='right'), MOE_EXPERTS - 1).astype(jnp.int32)
    block_e = jnp.where(blk < n_active, block_e, block_e[jnp.maximum(n_active - 1, 0)])
    pos = jnp.zeros((n_asg,), jnp.int32).at[order].set(dest)
    return block_e, n_active.reshape(1), buf_tok, buf_w, pos


def _moe_res_kernel(h_ref, y0_ref, y1_ref, mod_ref, o_ref, *, t_ctx, d_model):
    tm = h_ref.shape[0]
    is_ctx = _row_is_ctx(pl.program_id(0), tm, t_ctx)
    gate = _mod_rows(mod_ref, 5, d_model, is_ctx)
    o_ref[...] = h_ref[...] + gate * (y0_ref[...] + y1_ref[...])


def _moe_res(h, y0, y1, mod, layer, t_ctx):
    t_all, d_model = h.shape
    tm = 256
    row = pl.BlockSpec((tm, d_model), lambda i: (i, 0))
    return pl.pallas_call(
        functools.partial(_moe_res_kernel, t_ctx=t_ctx, d_model=d_model),
        grid=(t_all // tm,),
        in_specs=[row, row, row,
                  pl.BlockSpec((None, SUBLANES, N_MOD * d_model), lambda i: (layer, 0, 0))],
        out_specs=row,
        out_shape=jax.ShapeDtypeStruct((t_all, d_model), F32),
        compiler_params=_params(1),
        name="moe_res",
    )(h, y0, y1, mod)


def _final_norm_kernel(h_ref, g_ref, o_ref):
    x = h_ref[...]
    o_ref[...] = x * lax.rsqrt(jnp.mean(x * x, axis=-1, keepdims=True) + EPS) * g_ref[...]


def _final_norm(h, g, t_ctx):
    t_all, d_model = h.shape
    tm = 256
    off = t_ctx // tm
    return pl.pallas_call(
        _final_norm_kernel,
        grid=((t_all - t_ctx) // tm,),
        in_specs=[pl.BlockSpec((tm, d_model), lambda i: (i + off, 0)),
                  pl.BlockSpec((1, d_model), lambda i: (0, 0))],
        out_specs=pl.BlockSpec((tm, d_model), lambda i: (i, 0)),
        out_shape=jax.ShapeDtypeStruct((t_all - t_ctx, d_model), F32),
        compiler_params=_params(1),
        name="final_norm",
    )(h, g.reshape(1, d_model))


def kernel(x, c, ctx, c_ctx, w_ada, b_ada, g_mix, w_in, b_gate, ln_a_g, ln_a_b, w_sp, b_sp, conv_w, conv_b, dt_bias, a_log, d_skip, g_ssm, w_pa, w_pb, w_pc, w_out, g_ffn, w_rg, b_rg, w_re, b_re, w_e1, w_e3, w_e2, g_final):
    bsz, t_lat, d_model = x.shape
    assert bsz == 1 and c.shape[0] == 1
    t_ctx = ctx.shape[1]
    depth = w_ada.shape[0]
    a_width = ln_a_g.shape[1]
    inner = SSM_HEADS * SSM_HEAD_DIM
    conv_dim = conv_w.shape[2]
    f_width = w_pc.shape[1]
    f_gd = f_width // F_GROUPS
    n_ctx_chunks = t_ctx // CHUNK
    assert t_ctx % 256 == 0 and t_lat % 256 == 0 and t_lat % (CHUNK * SUBLANES) == 0

    c_u, c_z = 0, 2 * a_width
    c_xbc = c_z + inner
    c_dt = c_xbc + conv_dim
    c_fc = c_dt + 2 * SSM_HEADS
    c_gl = c_fc + f_width
    n_gl = N_BRANCH * d_model
    p_z, p_xbc, p_gl = 2 * a_width, 2 * a_width + inner, 2 * a_width + inner + conv_dim

    cs = jnp.zeros((SUBLANES, d_model), F32).at[0].set(c[0]).at[1].set(c_ctx)
    mod = _ada(cs, w_ada, b_ada)

    h = jnp.concatenate([ctx[0], x[0]], axis=0)
    pad_heads = jnp.zeros((128 - 2 * SSM_HEADS,), F32)

    for l in range(depth):
        wl = w_in[l]
        w_main = jnp.concatenate([wl[:, c_u:c_dt], wl[:, c_gl:c_gl + n_gl]], axis=1).astype(BF16)
        w_dt = jnp.concatenate([wl[:, c_dt:c_fc], jnp.zeros((d_model, 128 - 2 * SSM_HEADS), F32)], axis=1).astype(BF16)
        w_fc = wl[:, c_fc:c_gl].astype(BF16)

        hn = _norm_mod(h, g_mix[l], mod, l, 0, 1, t_ctx, BF16)
        proj = _mm(hn, w_main, F32, 1024, "proj_main")
        dt_raw = _mm(hn, w_dt, F32, 128, "proj_dt")
        fc = _mm(hn, w_fc, BF16, 1024, "proj_fc")

        ya = _sgu(proj, ln_a_g[l], ln_a_b[l], w_sp[l], b_sp[l], a_width)

        dtb = jnp.concatenate([dt_bias[l].reshape(-1), pad_heads]).reshape(1, 128)
        alg = jnp.concatenate([a_log[l].reshape(-1), pad_heads]).reshape(1, 128)
        dsk = jnp.repeat(d_skip[l], SSM_HEAD_DIM, axis=-1)
        assert p_xbc % conv_dim == 0 and p_z % inner == 0
        y_f = _ssd(proj, p_xbc // conv_dim, dt_raw, conv_w[l], conv_b[l], dtb, alg, dsk[0:1], 0, n_ctx_chunks)
        yb = _ssd(proj, p_xbc // conv_dim, dt_raw, conv_w[l], conv_b[l], dtb, alg, dsk[1:2], 1, n_ctx_chunks,
                  yprev=y_f, z_blk0=p_z // inner, g_ssm=g_ssm[l])

        yc_lat = _fourier_long(fc[t_ctx:], f_gd)
        if l < depth - 1:
            yc_ctx = _fourier_short(fc[:t_ctx], f_gd)
        else:
            yc_ctx = jnp.zeros((t_ctx, f_width), BF16)
        yc = jnp.concatenate([yc_ctx, yc_lat], axis=0)

        merged = _merge(ya, yb, yc, proj, p_gl, b_gate[l], w_pa[l], w_pb[l], w_pc[l])
        h = _proj_res(merged, w_out[l], h, mod, l, 2, t_ctx)

        w_router = jnp.concatenate([w_rg[l], w_re[l], jnp.zeros((d_model, ROUTER_COLS - MOE_GROUPS - MOE_EXPERTS), F32)], axis=1)
        b_router = jnp.concatenate([b_rg[l], b_re[l], jnp.zeros((ROUTER_COLS - MOE_GROUPS - MOE_EXPERTS,), F32)]).reshape(1, ROUTER_COLS)
        xn, logits = _ffn_norm(h, g_ffn[l], mod, l, w_router, b_router, t_ctx)
        block_e, n_active, buf_tok, buf_w, pos = _route(logits)
        x_sorted = jnp.take(xn, buf_tok, axis=0)
        yblk = _moe_blocks(block_e, n_active, x_sorted, w_e1[l], w_e3[l], w_e2[l], buf_w)
        y0 = jnp.take(yblk, pos[0::2], axis=0)
        y1 = jnp.take(yblk, pos[1::2], axis=0)
        h = _moe_res(h, y0, y1, mod, l, t_ctx)

    out = _final_norm(h, g_final, t_ctx)
    return out.reshape(bsz, t_lat, d_model)
```

```python
import functools

import numpy as np
import jax
import jax.numpy as jnp
from jax import lax
from jax.experimental import pallas as pl
from jax.experimental.pallas import tpu as pltpu

F32 = jnp.float32
BF16 = jnp.bfloat16
HIGHEST = lax.Precision.HIGHEST

EPS = 1e-6
N_MOD = 6
GRID_W = 64
CHUNK = 128
A_GROUPS = 8
SSM_HEADS = 32
SSM_HEAD_DIM = 64
SSM_GROUPS = 8
SSM_STATE = 128
F_GROUPS = 4
N_BRANCH = 3
MOE_GROUPS = 8
MOE_EPG = 8
MOE_EXPERTS = 64
MOE_TOPK = 2
MOE_BLOCK = 128
ROUTER_COLS = 128

V7X_VMEM_LIMIT_BYTES = 56 * 1024 * 1024
SUBLANES = 8


def _params(n_axes):
    return pltpu.CompilerParams(dimension_semantics=("arbitrary",) * n_axes,
                                vmem_limit_bytes=V7X_VMEM_LIMIT_BYTES)


def _pick_tile(n, candidates):
    for t in candidates:
        if n % t == 0:
            return t
    raise ValueError(f"no tile for {n}")


def _gelu(x):
    return 0.5 * x * (1.0 + lax.erf(x * np.float32(1.0 / np.sqrt(2.0))))


def _silu(x):
    return x * jax.nn.sigmoid(x)


def _softplus(x):
    return jnp.maximum(x, 0.0) + jnp.log1p(jnp.exp(-jnp.abs(x)))


def _row_is_ctx(tile_index, tm, t_ctx):
    rows = tile_index * tm + lax.broadcasted_iota(jnp.int32, (tm, 1), 0)
    return rows < t_ctx


def _mod_rows(mod_ref, k, d_model, is_ctx):
    lo = k * d_model
    lat = mod_ref[0:1, lo:lo + d_model]
    ctx = mod_ref[1:2, lo:lo + d_model]
    return jnp.where(is_ctx, ctx, lat)


def _ada_kernel(cs_ref, w_ref, b_ref, o_ref):
    s = _silu(cs_ref[...])
    o_ref[...] = jnp.dot(s, w_ref[...], precision=HIGHEST, preferred_element_type=F32) + b_ref[...]


def _ada(cs, w_ada, b_ada):
    depth, d_model, n = w_ada.shape
    tn = 1024
    return pl.pallas_call(
        _ada_kernel,
        grid=(depth, n // tn),
        in_specs=[pl.BlockSpec((SUBLANES, d_model), lambda l, j: (0, 0)),
                  pl.BlockSpec((None, d_model, tn), lambda l, j: (l, 0, j)),
                  pl.BlockSpec((None, 1, tn), lambda l, j: (l, 0, j))],
        out_specs=pl.BlockSpec((None, SUBLANES, tn), lambda l, j: (l, 0, j)),
        out_shape=jax.ShapeDtypeStruct((depth, SUBLANES, n), F32),
        compiler_params=_params(2),
        name="ada",
    )(cs, w_ada, b_ada.reshape(depth, 1, n))


def _norm_mod_kernel(h_ref, g_ref, mod_ref, o_ref, *, k_shift, k_scale, t_ctx):
    tm, d_model = h_ref.shape
    x = h_ref[...]
    y = x * lax.rsqrt(jnp.mean(x * x, axis=-1, keepdims=True) + EPS) * g_ref[...]
    is_ctx = _row_is_ctx(pl.program_id(0), tm, t_ctx)
    shift = _mod_rows(mod_ref, k_shift, d_model, is_ctx)
    scale = _mod_rows(mod_ref, k_scale, d_model, is_ctx)
    o_ref[...] = (y * (1.0 + scale) + shift).astype(o_ref.dtype)


def _norm_mod(h, g, mod, layer, k_shift, k_scale, t_ctx, out_dtype):
    t_all, d_model = h.shape
    tm = 256
    return pl.pallas_call(
        functools.partial(_norm_mod_kernel, k_shift=k_shift, k_scale=k_scale, t_ctx=t_ctx),
        grid=(t_all // tm,),
        in_specs=[pl.BlockSpec((tm, d_model), lambda i: (i, 0)),
                  pl.BlockSpec((1, d_model), lambda i: (0, 0)),
                  pl.BlockSpec((None, SUBLANES, N_MOD * d_model), lambda i: (layer, 0, 0))],
        out_specs=pl.BlockSpec((tm, d_model), lambda i: (i, 0)),
        out_shape=jax.ShapeDtypeStruct((t_all, d_model), out_dtype),
        compiler_params=_params(1),
        name="norm_mod",
    )(h, g.reshape(1, d_model), mod)


def _mm_kernel(a_ref, w_ref, o_ref):
    o_ref[...] = jnp.dot(a_ref[...], w_ref[...], preferred_element_type=F32).astype(o_ref.dtype)


def _mm(a, w, out_dtype, tn, name):
    m, k = a.shape
    n = w.shape[1]
    tm = _pick_tile(m, (1280, 1024, 640, 512, 256, 128))
    return pl.pallas_call(
        _mm_kernel,
        grid=(n // tn, m // tm),
        in_specs=[pl.BlockSpec((tm, k), lambda j, i: (i, 0)),
                  pl.BlockSpec((k, tn), lambda j, i: (0, j))],
        out_specs=pl.BlockSpec((tm, tn), lambda j, i: (i, j)),
        out_shape=jax.ShapeDtypeStruct((m, n), out_dtype),
        compiler_params=_params(2),
        name=name,
    )(a, w)


def _sgu_kernel(u_ref, v_ref, g_ref, b_ref, wsp_ref, bsp_ref, o_ref):
    v = _gelu(v_ref[...])
    mu = jnp.mean(v, axis=-1, keepdims=True)
    var = jnp.mean(jnp.square(v - mu), axis=-1, keepdims=True)
    vn = ((v - mu) * lax.rsqrt(var + EPS)) * g_ref[...] + b_ref[...]
    vb = vn.astype(BF16)
    gd = v.shape[-1] // A_GROUPS
    for g in range(A_GROUPS):
        cols = slice(g * gd, (g + 1) * gd)
        s = jnp.dot(wsp_ref[g], vb[:, cols], preferred_element_type=F32) + bsp_ref[:, g:g + 1]
        o_ref[:, cols] = (_gelu(u_ref[:, cols]) * s).astype(o_ref.dtype)


def _sgu(proj, ln_g, ln_b, w_sp, b_sp, a_width):
    t_all = proj.shape[0]
    return pl.pallas_call(
        _sgu_kernel,
        grid=(t_all // CHUNK,),
        in_specs=[pl.BlockSpec((CHUNK, a_width), lambda c: (c, 0)),
                  pl.BlockSpec((CHUNK, a_width), lambda c: (c, 1)),
                  pl.BlockSpec((1, a_width), lambda c: (0, 0)),
                  pl.BlockSpec((1, a_width), lambda c: (0, 0)),
                  pl.BlockSpec((A_GROUPS, CHUNK, CHUNK), lambda c: (0, 0, 0)),
                  pl.BlockSpec((CHUNK, A_GROUPS), lambda c: (0, 0))],
        out_specs=pl.BlockSpec((CHUNK, a_width), lambda c: (c, 0)),
        out_shape=jax.ShapeDtypeStruct((t_all, a_width), BF16),
        compiler_params=_params(1),
        name="sgu",
    )(proj, proj, ln_g.reshape(1, a_width), ln_b.reshape(1, a_width), w_sp.astype(BF16), b_sp.T)


def _ssd_chunk_of(i, direction, n_ctx_chunks, n_chunks):
    if direction == 0:
        return i
    return jnp.where(i < n_ctx_chunks, n_ctx_chunks - 1 - i, n_chunks - 1 - (i - n_ctx_chunks))


def _ssd_kernel(*refs, direction, n_ctx_chunks, n_chunks, final):
    if final:
        (xbc_ref, prev_ref, next_ref, dt_ref, cw_ref, cb_ref, dtb_ref, alog_ref, dsk_ref,
         yprev_ref, z_ref, gs_ref, o_ref, state_ref, t_ref) = refs
    else:
        (xbc_ref, prev_ref, next_ref, dt_ref, cw_ref, cb_ref, dtb_ref, alog_ref, dsk_ref,
         o_ref, state_ref) = refs
    L = CHUNK
    i = pl.program_id(0)
    c = _ssd_chunk_of(i, direction, n_ctx_chunks, n_chunks)
    is_first = jnp.logical_or(c == 0, c == n_ctx_chunks)
    is_last = jnp.logical_or(c == n_ctx_chunks - 1, c == n_chunks - 1)

    @pl.when(i == 0)
    def _():
        state_ref[...] = jnp.zeros_like(state_ref)

    row = lax.broadcasted_iota(jnp.int32, (L, L), 0)
    col = lax.broadcasted_iota(jnp.int32, (L, L), 1)
    causal = (col <= row) if direction == 0 else (col >= row)
    row1 = lax.broadcasted_iota(jnp.int32, (L, 1), 0)

    dt = _softplus(dt_ref[...] + dtb_ref[...])
    a = -jnp.exp(alog_ref[...])
    acs = jnp.dot(causal.astype(F32), dt * a, precision=HIGHEST, preferred_element_type=F32)
    acs_end = acs[L - 1:L, :] if direction == 0 else acs[0:1, :]
    m_all = dt * jnp.exp(acs_end - acs)
    e_in = jnp.exp(acs)
    chunk_decay = jnp.exp(acs_end)
    acs_t = acs.T
    dt_t = dt.T

    def conv_silu(c0, w):
        x = xbc_ref[:, c0:c0 + w]
        pr = jnp.where(is_first, 0.0, prev_ref[SUBLANES - 1:SUBLANES, c0:c0 + w])
        nx = jnp.where(is_last, 0.0, next_ref[0:1, c0:c0 + w])
        xm = jnp.where(row1 == 0, pr, pltpu.roll(x, 1, 0))
        xp = jnp.where(row1 == L - 1, nx, pltpu.roll(x, L - 1, 0))
        y = (xm * cw_ref[0:1, c0:c0 + w] + x * cw_ref[1:2, c0:c0 + w]
             + xp * cw_ref[2:3, c0:c0 + w] + cb_ref[:, c0:c0 + w])
        return _silu(y)

    inner = SSM_HEADS * SSM_HEAD_DIM
    bc = SSM_GROUPS * SSM_STATE
    pair_w = 2 * SSM_HEAD_DIM
    lane = lax.broadcasted_iota(jnp.int32, (L, pair_w), 1)
    lo_half = lane < SSM_HEAD_DIM
    prow = lax.broadcasted_iota(jnp.int32, (pair_w, 1), 0)
    lo_rows = prow < SSM_HEAD_DIM
    ssq = jnp.zeros((L, 1), F32)

    for g in range(SSM_GROUPS):
        b_g = conv_silu(inner + g * SSM_STATE, SSM_STATE)
        c_g = conv_silu(inner + bc + g * SSM_STATE, SSM_STATE)
        cb = lax.dot_general(c_g.astype(BF16), b_g.astype(BF16), (((1,), (1,)), ((), ())),
                             preferred_element_type=F32)
        for jj in range(2):
            j = 2 * g + jj
            cols = [direction * SSM_HEADS + 2 * j, direction * SSM_HEADS + 2 * j + 1]
            x2f = conv_silu(j * pair_w, pair_w)
            x2 = x2f.astype(BF16)
            zero = jnp.zeros_like(x2)
            xd = jnp.concatenate([jnp.where(lo_half, x2, zero), jnp.where(lo_half, zero, x2)], axis=0)
            l_parts, c_parts, b_parts = [], [], []
            for cc in cols:
                seg = acs[:, cc:cc + 1] - acs_t[cc:cc + 1, :]
                dec = jnp.exp(jnp.where(causal, seg, -jnp.inf))
                l_parts.append((cb * dec * dt_t[cc:cc + 1, :]).astype(BF16))
                c_parts.append((c_g * e_in[:, cc:cc + 1]).astype(BF16))
                b_parts.append((b_g * m_all[:, cc:cc + 1]).astype(BF16))
            l2 = jnp.concatenate(l_parts, axis=1)
            c2 = jnp.concatenate(c_parts, axis=1)
            b2 = jnp.concatenate(b_parts, axis=0)
            sp = state_ref[j]
            spb = sp.astype(BF16)
            zs = jnp.zeros_like(spb)
            sd = jnp.concatenate([jnp.where(lo_rows, spb, zs), jnp.where(lo_rows, zs, spb)], axis=1)
            y = jnp.dot(l2, xd, preferred_element_type=F32)
            y = y + lax.dot_general(c2, sd, (((1,), (1,)), ((), ())), preferred_element_type=F32)
            y = y + dsk_ref[:, j * pair_w:(j + 1) * pair_w] * x2f
            upd = lax.dot_general(xd, b2, (((0,), (0,)), ((), ())), preferred_element_type=F32)
            cd = jnp.where(lo_rows, chunk_decay[:, cols[0]:cols[0] + 1], chunk_decay[:, cols[1]:cols[1] + 1])
            state_ref[j] = cd * sp + upd
            ysl = slice(j * pair_w, (j + 1) * pair_w)
            if final:
                t = (yprev_ref[:, ysl] + y) * _silu(z_ref[:, ysl])
                ssq = ssq + jnp.sum(t * t, axis=-1, keepdims=True)
                t_ref[:, ysl] = t
            else:
                o_ref[:, ysl] = y
    if final:
        scale = lax.rsqrt(ssq * np.float32(1.0 / inner) + EPS)
        o_ref[...] = (t_ref[...] * scale * gs_ref[...]).astype(o_ref.dtype)


def _ssd(proj, xbc_blk0, dt_raw, conv_w, conv_b, dt_bias_p, a_log_p, d_skip_x, direction,
         n_ctx_chunks, yprev=None, z_blk0=None, g_ssm=None):
    t_all = proj.shape[0]
    n_chunks = t_all // CHUNK
    inner = SSM_HEADS * SSM_HEAD_DIM
    conv_dim = conv_w.shape[1]
    final = yprev is not None
    chunk_of = functools.partial(_ssd_chunk_of, direction=direction, n_ctx_chunks=n_ctx_chunks,
                                 n_chunks=n_chunks)
    rows8 = CHUNK // SUBLANES
    last8 = t_all // SUBLANES - 1
    in_specs = [
        pl.BlockSpec((CHUNK, conv_dim), lambda i: (chunk_of(i), xbc_blk0)),
        pl.BlockSpec((SUBLANES, conv_dim), lambda i: (jnp.maximum(chunk_of(i) * rows8 - 1, 0), xbc_blk0)),
        pl.BlockSpec((SUBLANES, conv_dim), lambda i: (jnp.minimum(chunk_of(i) * rows8 + rows8, last8), xbc_blk0)),
        pl.BlockSpec((CHUNK, 128), lambda i: (chunk_of(i), 0)),
        pl.BlockSpec((3, conv_dim), lambda i: (0, 0)),
        pl.BlockSpec((1, conv_dim), lambda i: (0, 0)),
        pl.BlockSpec((1, 128), lambda i: (0, 0)),
        pl.BlockSpec((1, 128), lambda i: (0, 0)),
        pl.BlockSpec((1, inner), lambda i: (0, 0)),
    ]
    args = [proj, proj, proj, dt_raw, conv_w, conv_b.reshape(1, conv_dim), dt_bias_p, a_log_p, d_skip_x]
    scratch = [pltpu.VMEM((SSM_HEADS // 2, 2 * SSM_HEAD_DIM, SSM_STATE), F32)]
    if final:
        in_specs += [pl.BlockSpec((CHUNK, inner), lambda i: (chunk_of(i), 0)),
                     pl.BlockSpec((CHUNK, inner), lambda i: (chunk_of(i), z_blk0)),
                     pl.BlockSpec((1, inner), lambda i: (0, 0))]
        args += [yprev, proj, g_ssm.reshape(1, inner)]
        scratch.append(pltpu.VMEM((CHUNK, inner), F32))
    return pl.pallas_call(
        functools.partial(_ssd_kernel, direction=direction, n_ctx_chunks=n_ctx_chunks,
                          n_chunks=n_chunks, final=final),
        grid=(n_chunks,),
        in_specs=in_specs,
        out_specs=pl.BlockSpec((CHUNK, inner), lambda i: (chunk_of(i), 0)),
        out_shape=jax.ShapeDtypeStruct((t_all, inner), BF16 if final else F32),
        scratch_shapes=scratch,
        compiler_params=_params(1),
        name="ssd_bwd" if final else "ssd_fwd",
    )(*args)


def _dft_tables(n):
    k = np.arange(n, dtype=np.int64)
    ang = 2.0 * np.pi * ((k[:, None] * k[None, :]) % n).astype(np.float64) / n
    return np.cos(ang), np.sin(ang)


def _chan_dft_kernel(a_ref, w_ref, o_ref):
    o_ref[...] = jnp.dot(a_ref[...], w_ref[...], preferred_element_type=F32).astype(o_ref.dtype)


def _chan_dft(f, gd):
    rows, width = f.shape
    groups = width // gd
    cc, sc = _dft_tables(gd)
    w = jnp.asarray(np.concatenate([cc, -sc], axis=1), BF16)
    tm = _pick_tile(rows, (1024, 512, 256, 128))
    return pl.pallas_call(
        _chan_dft_kernel,
        grid=(rows // tm, groups),
        in_specs=[pl.BlockSpec((tm, gd), lambda i, g: (i, g)),
                  pl.BlockSpec((gd, 2 * gd), lambda i, g: (0, 0))],
        out_specs=pl.BlockSpec((tm, 2 * gd), lambda i, g: (i, g)),
        out_shape=jax.ShapeDtypeStruct((rows, 2 * width), BF16),
        compiler_params=_params(2),
        name="fnet_chan",
    )(f, w)


def _pos_dft1_kernel(y_ref, w_ref, tc_ref, ts_ref, o_ref, *, groups, gd):
    n1 = y_ref.shape[0]
    p = jnp.dot(w_ref[...], y_ref[...], preferred_element_type=F32)
    tc = tc_ref[...]
    ts = ts_ref[...]
    for g in range(groups):
        re = slice(g * 2 * gd, g * 2 * gd + gd)
        im = slice(g * 2 * gd + gd, (g + 1) * 2 * gd)
        zr = p[0:n1, re] - p[n1:2 * n1, im]
        zi = p[0:n1, im] + p[n1:2 * n1, re]
        o_ref[:, re] = (zr * tc + zi * ts).astype(o_ref.dtype)
        o_ref[:, im] = (zi * tc - zr * ts).astype(o_ref.dtype)


def _pos_dft2_kernel(z_ref, w_ref, o_ref, *, groups, gd, scale):
    for g in range(groups):
        re = slice(g * 2 * gd, g * 2 * gd + gd)
        im = slice(g * 2 * gd + gd, (g + 1) * 2 * gd)
        zz = jnp.concatenate([z_ref[:, re], z_ref[:, im]], axis=0)
        out = jnp.dot(w_ref[...], zz, preferred_element_type=F32)
        o_ref[:, g * gd:(g + 1) * gd] = (out * scale).astype(o_ref.dtype)


def _pos_dft2(z, n2, groups, gd, scale):
    rows = z.shape[0]
    c2, s2 = _dft_tables(n2)
    w2 = jnp.asarray(np.concatenate([c2, s2], axis=1), BF16)
    return pl.pallas_call(
        functools.partial(_pos_dft2_kernel, groups=groups, gd=gd, scale=np.float32(scale)),
        grid=(rows // n2,),
        in_specs=[pl.BlockSpec((n2, groups * 2 * gd), lambda i: (i, 0)),
                  pl.BlockSpec((n2, 2 * n2), lambda i: (0, 0))],
        out_specs=pl.BlockSpec((n2, groups * gd), lambda i: (i, 0)),
        out_shape=jax.ShapeDtypeStruct((rows, groups * gd), BF16),
        compiler_params=_params(1),
        name="fnet_pos2",
    )(z, w2)


def _fourier_long(f, gd):
    length, width = f.shape
    groups = width // gd
    n2 = CHUNK
    n1 = length // n2
    cw = groups * 2 * gd
    y = _chan_dft(f, gd)
    yt = y.reshape(n1, n2, cw).transpose(1, 0, 2).reshape(length, cw)
    c1, s1 = _dft_tables(n1)
    w1 = jnp.asarray(np.concatenate([c1, -s1], axis=0), BF16)
    ang = 2.0 * np.pi * (np.arange(n2)[:, None] * np.arange(n1)[None, :]).astype(np.float64) / length
    tc = jnp.asarray(np.cos(ang)[:, :, None], F32)
    ts = jnp.asarray(np.sin(ang)[:, :, None], F32)
    zt = pl.pallas_call(
        functools.partial(_pos_dft1_kernel, groups=groups, gd=gd),
        grid=(n2,),
        in_specs=[pl.BlockSpec((n1, cw), lambda i: (i, 0)),
                  pl.BlockSpec((2 * n1, n1), lambda i: (0, 0)),
                  pl.BlockSpec((None, n1, 1), lambda i: (i, 0, 0)),
                  pl.BlockSpec((None, n1, 1), lambda i: (i, 0, 0))],
        out_specs=pl.BlockSpec((n1, cw), lambda i: (i, 0)),
        out_shape=jax.ShapeDtypeStruct((length, cw), BF16),
        compiler_params=_params(1),
        name="fnet_pos1",
    )(yt, w1, tc, ts)
    z = zt.reshape(n2, n1, cw).transpose(1, 0, 2).reshape(length, cw)
    o = _pos_dft2(z, n2, groups, gd, 1.0 / np.sqrt(float(length) * gd))
    return o.reshape(n1, n2, width).transpose(1, 0, 2).reshape(length, width)


def _fourier_short(f, gd):
    length, width = f.shape
    groups = width // gd
    y = _chan_dft(f, gd)
    return _pos_dft2(y, length, groups, gd, 1.0 / np.sqrt(float(length) * gd))


def _merge_kernel(ya_ref, yb_ref, yc_ref, g0_ref, g1_ref, g2_ref, bg_ref, wa_ref, wb_ref, wc_ref, o_ref):
    tn = o_ref.shape[1]
    acc = jax.nn.sigmoid(g0_ref[...] + bg_ref[0:1, :]) * jnp.dot(ya_ref[...], wa_ref[...], preferred_element_type=F32)
    acc = acc + jax.nn.sigmoid(g1_ref[...] + bg_ref[1:2, :]) * jnp.dot(yb_ref[...], wb_ref[...], preferred_element_type=F32)
    acc = acc + jax.nn.sigmoid(g2_ref[...] + bg_ref[2:3, :]) * jnp.dot(yc_ref[...], wc_ref[...], preferred_element_type=F32)
    o_ref[...] = acc.astype(o_ref.dtype)


def _merge(ya, yb, yc, proj, gl_col0, b_gate, w_pa, w_pb, w_pc):
    t_all = ya.shape[0]
    d_model = w_pa.shape[1]
    tn = 512
    tm = _pick_tile(t_all, (640, 512, 256, 128))
    gblk0 = gl_col0 // tn
    nblk = d_model // tn
    return pl.pallas_call(
        _merge_kernel,
        grid=(d_model // tn, t_all // tm),
        in_specs=[pl.BlockSpec((tm, ya.shape[1]), lambda j, i: (i, 0)),
                  pl.BlockSpec((tm, yb.shape[1]), lambda j, i: (i, 0)),
                  pl.BlockSpec((tm, yc.shape[1]), lambda j, i: (i, 0)),
                  pl.BlockSpec((tm, tn), lambda j, i: (i, gblk0 + j)),
                  pl.BlockSpec((tm, tn), lambda j, i: (i, gblk0 + nblk + j)),
                  pl.BlockSpec((tm, tn), lambda j, i: (i, gblk0 + 2 * nblk + j)),
                  pl.BlockSpec((N_BRANCH, tn), lambda j, i: (0, j)),
                  pl.BlockSpec((w_pa.shape[0], tn), lambda j, i: (0, j)),
                  pl.BlockSpec((w_pb.shape[0], tn), lambda j, i: (0, j)),
                  pl.BlockSpec((w_pc.shape[0], tn), lambda j, i: (0, j))],
        out_specs=pl.BlockSpec((tm, tn), lambda j, i: (i, j)),
        out_shape=jax.ShapeDtypeStruct((t_all, d_model), BF16),
        compiler_params=_params(2),
        name="merge",
    )(ya, yb, yc, proj, proj, proj, b_gate.reshape(N_BRANCH, d_model),
      w_pa.astype(BF16), w_pb.astype(BF16), w_pc.astype(BF16))


def _proj_res_kernel(a_ref, w_ref, h_ref, mod_ref, o_ref, *, k_gate, t_ctx, d_model):
    tm, tn = o_ref.shape
    j = pl.program_id(0)
    is_ctx = _row_is_ctx(pl.program_id(1), tm, t_ctx)
    acc = jnp.dot(a_ref[...], w_ref[...], preferred_element_type=F32)
    gate = jnp.where(is_ctx, mod_ref[1:2, :], mod_ref[0:1, :])
    o_ref[...] = h_ref[...] + gate * acc


def _proj_res(a, w, h, mod, layer, k_gate, t_ctx):
    t_all, d_model = h.shape
    tn = 512
    tm = _pick_tile(t_all, (640, 512, 256, 128))
    nblk = d_model // tn
    return pl.pallas_call(
        functools.partial(_proj_res_kernel, k_gate=k_gate, t_ctx=t_ctx, d_model=d_model),
        grid=(d_model // tn, t_all // tm),
        in_specs=[pl.BlockSpec((tm, a.shape[1]), lambda j, i: (i, 0)),
                  pl.BlockSpec((a.shape[1], tn), lambda j, i: (0, j)),
                  pl.BlockSpec((tm, tn), lambda j, i: (i, j)),
                  pl.BlockSpec((None, SUBLANES, tn), lambda j, i: (layer, 0, k_gate * nblk + j))],
        out_specs=pl.BlockSpec((tm, tn), lambda j, i: (i, j)),
        out_shape=jax.ShapeDtypeStruct((t_all, d_model), F32),
        compiler_params=_params(2),
        name="proj_res",
    )(a, w.astype(BF16), h, mod)


ROUTE_EID0, ROUTE_EID1, ROUTE_GATE0, ROUTE_GATE1 = 0, 1, 2, 3


def _ffn_norm_kernel(h_ref, g_ref, mod_ref, wr_ref, br_ref, x_ref, rt_ref, *, t_ctx):
    tm, d_model = h_ref.shape
    x = h_ref[...]
    y = x * lax.rsqrt(jnp.mean(x * x, axis=-1, keepdims=True) + EPS) * g_ref[...]
    is_ctx = _row_is_ctx(pl.program_id(0), tm, t_ctx)
    shift = _mod_rows(mod_ref, 3, d_model, is_ctx)
    scale = _mod_rows(mod_ref, 4, d_model, is_ctx)
    xn = y * (1.0 + scale) + shift
    x_ref[...] = xn
    logits = jnp.dot(xn, wr_ref[...], precision=HIGHEST, preferred_element_type=F32) + br_ref[...]
    lane = lax.broadcasted_iota(jnp.int32, logits.shape, 1).astype(F32)
    far = np.float32(ROUTER_COLS)
    neg = np.float32(-np.inf)
    lg = jnp.where(lane < MOE_GROUPS, logits, neg)
    m = jnp.max(lg, axis=-1, keepdims=True)
    g_top = jnp.min(jnp.where(lg == m, lane, far), axis=-1, keepdims=True)
    p_sel = 1.0 / jnp.sum(jnp.exp(lg - m), axis=-1, keepdims=True)
    lo = MOE_GROUPS + g_top * MOE_EPG
    ls = jnp.where(jnp.logical_and(lane >= lo, lane < lo + MOE_EPG), logits, neg)
    v1 = jnp.max(ls, axis=-1, keepdims=True)
    i1 = jnp.min(jnp.where(ls == v1, lane, far), axis=-1, keepdims=True)
    ls2 = jnp.where(lane == i1, neg, ls)
    v2 = jnp.max(ls2, axis=-1, keepdims=True)
    i2 = jnp.min(jnp.where(ls2 == v2, lane, far), axis=-1, keepdims=True)
    e2 = jnp.exp(v2 - v1)
    den = 1.0 / (1.0 + e2)
    rec = jnp.where(lane == ROUTE_EID0, i1 - MOE_GROUPS, 0.0)
    rec = jnp.where(lane == ROUTE_EID1, i2 - MOE_GROUPS, rec)
    rec = jnp.where(lane == ROUTE_GATE0, p_sel * den, rec)
    rec = jnp.where(lane == ROUTE_GATE1, p_sel * (e2 * den), rec)
    rt_ref[...] = rec


def _ffn_norm(h, g, mod, layer, w_router, b_router, t_ctx):
    t_all, d_model = h.shape
    tm = 256
    return pl.pallas_call(
        functools.partial(_ffn_norm_kernel, t_ctx=t_ctx),
        grid=(t_all // tm,),
        in_specs=[pl.BlockSpec((tm, d_model), lambda i: (i, 0)),
                  pl.BlockSpec((1, d_model), lambda i: (0, 0)),
                  pl.BlockSpec((None, SUBLANES, N_MOD * d_model), lambda i: (layer, 0, 0)),
                  pl.BlockSpec((d_model, ROUTER_COLS), lambda i: (0, 0)),
                  pl.BlockSpec((1, ROUTER_COLS), lambda i: (0, 0))],
        out_specs=[pl.BlockSpec((tm, d_model), lambda i: (i, 0)),
                   pl.BlockSpec((tm, ROUTER_COLS), lambda i: (i, 0))],
        out_shape=[jax.ShapeDtypeStruct((t_all, d_model), F32),
                   jax.ShapeDtypeStruct((t_all, ROUTER_COLS), F32)],
        compiler_params=_params(1),
        name="ffn_norm",
    )(h, g.reshape(1, d_model), mod, w_router, b_router)


def _moe_kernel(be_ref, nact_ref, tok_ref, tok_next_ref, dst_ref, x_hbm, w1_ref, w3_ref, w2_ref, bw_ref,
                y_hbm, xbuf, ybuf, w1b, w3b, w2b, sem_in, sem_out):
    b = pl.program_id(0)
    n_active = nact_ref[0]
    slot = b % 2

    def gather_copy(row_index, r, s):
        return pltpu.make_async_copy(x_hbm.at[pl.ds(row_index, 1), :], xbuf.at[s, pl.ds(r, 1), :], sem_in.at[s])

    def scatter_copy(row_index, r, s):
        return pltpu.make_async_copy(ybuf.at[s, pl.ds(r, 1), :], y_hbm.at[pl.ds(row_index, 1), :], sem_out.at[s])

    def start_gather(idx_ref, s):
        def body(r, carry):
            gather_copy(idx_ref[0, r], r, s).start()
            return carry
        lax.fori_loop(0, MOE_BLOCK, body, 0, unroll=8)

    def wait_gather(s):
        def body(r, carry):
            gather_copy(0, r, s).wait()
            return carry
        lax.fori_loop(0, MOE_BLOCK, body, 0, unroll=8)

    def start_scatter(s):
        def body(r, carry):
            scatter_copy(dst_ref[0, r], r, s).start()
            return carry
        lax.fori_loop(0, MOE_BLOCK, body, 0, unroll=8)

    def wait_scatter(s):
        def body(r, carry):
            scatter_copy(0, r, s).wait()
            return carry
        lax.fori_loop(0, MOE_BLOCK, body, 0, unroll=8)

    @pl.when(b == 0)
    def _():
        start_gather(tok_ref, 0)

    @pl.when(b + 1 < n_active)
    def _():
        start_gather(tok_next_ref, 1 - slot)

    @pl.when(b < n_active)
    def _():
        new_expert = jnp.logical_or(b == 0, be_ref[b] != be_ref[jnp.maximum(b - 1, 0)])

        @pl.when(new_expert)
        def _():
            w1b[...] = w1_ref[...].astype(BF16)
            w3b[...] = w3_ref[...].astype(BF16)
            w2b[...] = w2_ref[...].astype(BF16)

        wait_gather(slot)

        @pl.when(b >= 2)
        def _():
            wait_scatter(slot)

        x = xbuf[slot].astype(BF16)
        h1 = jnp.dot(x, w1b[...], preferred_element_type=F32)
        h3 = jnp.dot(x, w3b[...], preferred_element_type=F32)
        hh = (_silu(h1) * h3).astype(BF16)
        ybuf[slot] = jnp.dot(hh, w2b[...], preferred_element_type=F32) * bw_ref[...]
        start_scatter(slot)

    @pl.when(b == n_active - 1)
    def _():
        @pl.when(b >= 1)
        def _():
            wait_scatter(1 - slot)
        wait_scatter(slot)
        ybuf[slot] = jnp.zeros(ybuf.shape[1:], F32)
        spare0 = y_hbm.shape[0] - 2 * MOE_BLOCK
        fills = [pltpu.make_async_copy(ybuf.at[slot], y_hbm.at[pl.ds(spare0 + k * MOE_BLOCK, MOE_BLOCK), :],
                                       sem_out.at[slot]) for k in range(2)]
        for cp in fills:
            cp.start()
        for cp in fills:
            cp.wait()


def _moe_blocks(layer, block_e, n_active, tok_idx, dst_idx, xn, w1, w3, w2, buf_w, n_out_rows):
    t_all, d_model = xn.shape
    ff = w1.shape[3]
    n_blocks = tok_idx.shape[0]

    def expert_blk(b, be, na):
        return (layer, be[b], 0, 0)

    idx_spec = functools.partial(pl.BlockSpec, (None, 1, MOE_BLOCK), memory_space=pltpu.SMEM)
    grid_spec = pltpu.PrefetchScalarGridSpec(
        num_scalar_prefetch=2,
        grid=(n_blocks,),
        in_specs=[idx_spec(index_map=lambda b, be, na: (b, 0, 0)),
                  idx_spec(index_map=lambda b, be, na: (jnp.minimum(b + 1, n_blocks - 1), 0, 0)),
                  idx_spec(index_map=lambda b, be, na: (b, 0, 0)),
                  pl.BlockSpec(memory_space=pl.ANY),
                  pl.BlockSpec((None, None, d_model, ff), expert_blk),
                  pl.BlockSpec((None, None, d_model, ff), expert_blk),
                  pl.BlockSpec((None, None, ff, d_model), expert_blk),
                  pl.BlockSpec((MOE_BLOCK, 1), lambda b, be, na: (b, 0))],
        out_specs=pl.BlockSpec(memory_space=pl.ANY),
        scratch_shapes=[pltpu.VMEM((2, MOE_BLOCK, d_model), F32), pltpu.VMEM((2, MOE_BLOCK, d_model), F32),
                        pltpu.VMEM((d_model, ff), BF16), pltpu.VMEM((d_model, ff), BF16),
                        pltpu.VMEM((ff, d_model), BF16),
                        pltpu.SemaphoreType.DMA((2,)), pltpu.SemaphoreType.DMA((2,))],
    )
    return pl.pallas_call(
        _moe_kernel,
        grid_spec=grid_spec,
        out_shape=jax.ShapeDtypeStruct((n_out_rows, d_model), F32),
        compiler_params=_params(1),
        name="moe_blocks",
    )(block_e, n_active, tok_idx, tok_idx, dst_idx, xn, w1, w3, w2, buf_w.reshape(-1, 1))


def _dispatch(route, t_all):
    eid = route[:, ROUTE_EID0:ROUTE_EID1 + 1].astype(jnp.int32)
    gate = route[:, ROUTE_GATE0:ROUTE_GATE1 + 1]
    n_asg = t_all * MOE_TOPK
    flat_e = eid.reshape(n_asg)
    flat_w = gate.reshape(n_asg)
    order = jnp.argsort(flat_e).astype(jnp.int32)
    experts = jnp.arange(MOE_EXPERTS, dtype=jnp.int32)
    counts = jnp.sum((flat_e[:, None] == experts[None, :]).astype(jnp.int32), axis=0)
    padded = (counts + MOE_BLOCK - 1) // MOE_BLOCK * MOE_BLOCK
    pad_end = jnp.cumsum(padded)
    pad_start = pad_end - padded
    start = jnp.cumsum(counts) - counts
    n_blocks = (n_asg + MOE_EXPERTS * (MOE_BLOCK - 1)) // MOE_BLOCK + 1
    blk = jnp.arange(n_blocks, dtype=jnp.int32)
    blk_e = jnp.minimum(jnp.sum((pad_end[None, :] <= (blk * MOE_BLOCK)[:, None]).astype(jnp.int32), axis=1),
                        MOE_EXPERTS - 1)
    n_active = pad_end[-1] // MOE_BLOCK
    block_e = jnp.where(blk < n_active, blk_e, blk_e[jnp.maximum(n_active - 1, 0)])
    r_in_blk = jnp.arange(MOE_BLOCK, dtype=jnp.int32)[None, :]
    r = (blk * MOE_BLOCK - pad_start[blk_e])[:, None] + r_in_blk
    valid = jnp.logical_and(r < counts[blk_e][:, None], (blk < n_active)[:, None])
    p = jnp.clip(start[blk_e][:, None] + r, 0, n_asg - 1)
    a = jnp.take(order, p, mode="clip")
    tok = jnp.where(valid, a >> 1, 0)
    spare = MOE_TOPK * t_all + (blk % 2)[:, None] * MOE_BLOCK + r_in_blk
    dst = jnp.where(valid, (a & 1) * t_all + (a >> 1), spare)
    w = jnp.where(valid, jnp.take(flat_w, a, mode="clip"), 0.0)
    return (block_e.astype(jnp.int32), n_active.astype(jnp.int32).reshape(1),
            tok.reshape(n_blocks, 1, MOE_BLOCK).astype(jnp.int32),
            dst.reshape(n_blocks, 1, MOE_BLOCK).astype(jnp.int32), w.reshape(-1))


def _moe_res_kernel(h_ref, y0_ref, y1_ref, mod_ref, o_ref, *, t_ctx, d_model):
    tm = h_ref.shape[0]
    is_ctx = _row_is_ctx(pl.program_id(0), tm, t_ctx)
    gate = _mod_rows(mod_ref, 5, d_model, is_ctx)
    o_ref[...] = h_ref[...] + gate * (y0_ref[...] + y1_ref[...])


def _moe_res(h, y01, mod, layer, t_ctx):
    t_all, d_model = h.shape
    tm = 256
    nt = t_all // tm
    row = pl.BlockSpec((tm, d_model), lambda i: (i, 0))
    return pl.pallas_call(
        functools.partial(_moe_res_kernel, t_ctx=t_ctx, d_model=d_model),
        grid=(nt,),
        in_specs=[row, row, pl.BlockSpec((tm, d_model), lambda i: (i + nt, 0)),
                  pl.BlockSpec((None, SUBLANES, N_MOD * d_model), lambda i: (layer, 0, 0))],
        out_specs=row,
        out_shape=jax.ShapeDtypeStruct((t_all, d_model), F32),
        compiler_params=_params(1),
        name="moe_res",
    )(h, y01, y01, mod)


def _final_norm_kernel(h_ref, g_ref, o_ref):
    x = h_ref[...]
    o_ref[...] = x * lax.rsqrt(jnp.mean(x * x, axis=-1, keepdims=True) + EPS) * g_ref[...]


def _final_norm(h, g, t_ctx):
    t_all, d_model = h.shape
    tm = 256
    off = t_ctx // tm
    return pl.pallas_call(
        _final_norm_kernel,
        grid=((t_all - t_ctx) // tm,),
        in_specs=[pl.BlockSpec((tm, d_model), lambda i: (i + off, 0)),
                  pl.BlockSpec((1, d_model), lambda i: (0, 0))],
        out_specs=pl.BlockSpec((tm, d_model), lambda i: (i, 0)),
        out_shape=jax.ShapeDtypeStruct((t_all - t_ctx, d_model), F32),
        compiler_params=_params(1),
        name="final_norm",
    )(h, g.reshape(1, d_model))


def kernel(x, c, ctx, c_ctx, w_ada, b_ada, g_mix, w_in, b_gate, ln_a_g, ln_a_b, w_sp, b_sp, conv_w, conv_b, dt_bias, a_log, d_skip, g_ssm, w_pa, w_pb, w_pc, w_out, g_ffn, w_rg, b_rg, w_re, b_re, w_e1, w_e3, w_e2, g_final):
    bsz, t_lat, d_model = x.shape
    assert bsz == 1 and c.shape[0] == 1
    t_ctx = ctx.shape[1]
    depth = w_ada.shape[0]
    a_width = ln_a_g.shape[1]
    inner = SSM_HEADS * SSM_HEAD_DIM
    conv_dim = conv_w.shape[2]
    f_width = w_pc.shape[1]
    f_gd = f_width // F_GROUPS
    n_ctx_chunks = t_ctx // CHUNK
    t_all = t_ctx + t_lat
    assert t_ctx % 256 == 0 and t_lat % 256 == 0 and t_lat % (CHUNK * SUBLANES) == 0

    c_u, c_z = 0, 2 * a_width
    c_xbc = c_z + inner
    c_dt = c_xbc + conv_dim
    c_fc = c_dt + 2 * SSM_HEADS
    c_gl = c_fc + f_width
    n_gl = N_BRANCH * d_model
    p_z, p_xbc, p_gl = 2 * a_width, 2 * a_width + inner, 2 * a_width + inner + conv_dim

    cs = jnp.zeros((SUBLANES, d_model), F32).at[0].set(c[0]).at[1].set(c_ctx)
    mod = _ada(cs, w_ada, b_ada)

    h = jnp.concatenate([ctx[0], x[0]], axis=0)
    pad_heads = jnp.zeros((128 - 2 * SSM_HEADS,), F32)

    for l in range(depth):
        wl = w_in[l]
        w_main = jnp.concatenate([wl[:, c_u:c_dt], wl[:, c_gl:c_gl + n_gl]], axis=1).astype(BF16)
        w_dt = jnp.concatenate([wl[:, c_dt:c_fc], jnp.zeros((d_model, 128 - 2 * SSM_HEADS), F32)], axis=1).astype(BF16)
        w_fc = wl[:, c_fc:c_gl].astype(BF16)

        hn = _norm_mod(h, g_mix[l], mod, l, 0, 1, t_ctx, BF16)
        proj = _mm(hn, w_main, F32, 1024, "proj_main")
        dt_raw = _mm(hn, w_dt, F32, 128, "proj_dt")
        fc = _mm(hn, w_fc, BF16, 1024, "proj_fc")

        ya = _sgu(proj, ln_a_g[l], ln_a_b[l], w_sp[l], b_sp[l], a_width)

        dtb = jnp.concatenate([dt_bias[l].reshape(-1), pad_heads]).reshape(1, 128)
        alg = jnp.concatenate([a_log[l].reshape(-1), pad_heads]).reshape(1, 128)
        dsk = jnp.repeat(d_skip[l], SSM_HEAD_DIM, axis=-1)
        assert p_xbc % conv_dim == 0 and p_z % inner == 0
        y_f = _ssd(proj, p_xbc // conv_dim, dt_raw, conv_w[l], conv_b[l], dtb, alg, dsk[0:1], 0, n_ctx_chunks)
        yb = _ssd(proj, p_xbc // conv_dim, dt_raw, conv_w[l], conv_b[l], dtb, alg, dsk[1:2], 1, n_ctx_chunks,
                  yprev=y_f, z_blk0=p_z // inner, g_ssm=g_ssm[l])

        yc_lat = _fourier_long(fc[t_ctx:], f_gd)
        if l < depth - 1:
            yc_ctx = _fourier_short(fc[:t_ctx], f_gd)
        else:
            yc_ctx = jnp.zeros((t_ctx, f_width), BF16)
        yc = jnp.concatenate([yc_ctx, yc_lat], axis=0)

        merged = _merge(ya, yb, yc, proj, p_gl, b_gate[l], w_pa[l], w_pb[l], w_pc[l])
        h = _proj_res(merged, w_out[l], h, mod, l, 2, t_ctx)

        w_router = jnp.concatenate([w_rg[l], w_re[l], jnp.zeros((d_model, ROUTER_COLS - MOE_GROUPS - MOE_EXPERTS), F32)], axis=1)
        b_router = jnp.concatenate([b_rg[l], b_re[l], jnp.zeros((ROUTER_COLS - MOE_GROUPS - MOE_EXPERTS,), F32)]).reshape(1, ROUTER_COLS)
        xn, route = _ffn_norm(h, g_ffn[l], mod, l, w_router, b_router, t_ctx)
        block_e, n_active, tok_idx, dst_idx, buf_w = _dispatch(route, t_all)
        y01 = _moe_blocks(l, block_e, n_active, tok_idx, dst_idx, xn, w_e1, w_e3, w_e2, buf_w,
                          MOE_TOPK * t_all + 2 * MOE_BLOCK)
        h = _moe_res(h, y01, mod, l, t_ctx)

    out = _final_norm(h, g_final, t_ctx)
    return out.reshape(bsz, t_lat, d_model)
```

```python
import functools

import numpy as np
import jax
import jax.numpy as jnp
from jax import lax
from jax.experimental import pallas as pl
from jax.experimental.pallas import tpu as pltpu

F32 = jnp.float32
BF16 = jnp.bfloat16
HIGHEST = lax.Precision.HIGHEST

EPS = 1e-6
N_MOD = 6
GRID_W = 64
CHUNK = 128
A_GROUPS = 8
SSM_HEADS = 32
SSM_HEAD_DIM = 64
SSM_GROUPS = 8
SSM_STATE = 128
F_GROUPS = 4
N_BRANCH = 3
MOE_GROUPS = 8
MOE_EPG = 8
MOE_EXPERTS = 64
MOE_TOPK = 2
MOE_BLOCK = 128
ROUTER_COLS = 128

V7X_VMEM_LIMIT_BYTES = 56 * 1024 * 1024
SUBLANES = 8


def _params(n_axes):
    return pltpu.CompilerParams(dimension_semantics=("arbitrary",) * n_axes,
                                vmem_limit_bytes=V7X_VMEM_LIMIT_BYTES)


def _pick_tile(n, candidates):
    for t in candidates:
        if n % t == 0:
            return t
    raise ValueError(f"no tile for {n}")


def _gelu(x):
    return 0.5 * x * (1.0 + lax.erf(x * np.float32(1.0 / np.sqrt(2.0))))


def _silu(x):
    return x * jax.nn.sigmoid(x)


def _softplus(x):
    return jnp.maximum(x, 0.0) + jnp.log1p(jnp.exp(-jnp.abs(x)))


def _row_is_ctx(tile_index, tm, t_ctx):
    rows = tile_index * tm + lax.broadcasted_iota(jnp.int32, (tm, 1), 0)
    return rows < t_ctx


def _mod_rows(mod_ref, k, d_model, is_ctx):
    lo = k * d_model
    lat = mod_ref[0:1, lo:lo + d_model]
    ctx = mod_ref[1:2, lo:lo + d_model]
    return jnp.where(is_ctx, ctx, lat)


def _ada_kernel(cs_ref, w_ref, b_ref, o_ref):
    s = _silu(cs_ref[...])
    o_ref[...] = jnp.dot(s, w_ref[...], precision=HIGHEST, preferred_element_type=F32) + b_ref[...]


def _ada(cs, w_ada, b_ada):
    depth, d_model, n = w_ada.shape
    tn = 1024
    return pl.pallas_call(
        _ada_kernel,
        grid=(depth, n // tn),
        in_specs=[pl.BlockSpec((SUBLANES, d_model), lambda l, j: (0, 0)),
                  pl.BlockSpec((None, d_model, tn), lambda l, j: (l, 0, j)),
                  pl.BlockSpec((None, 1, tn), lambda l, j: (l, 0, j))],
        out_specs=pl.BlockSpec((None, SUBLANES, tn), lambda l, j: (l, 0, j)),
        out_shape=jax.ShapeDtypeStruct((depth, SUBLANES, n), F32),
        compiler_params=_params(2),
        name="ada",
    )(cs, w_ada, b_ada.reshape(depth, 1, n))


def _norm_mod_kernel(h_ref, g_ref, mod_ref, o_ref, *, k_shift, k_scale, t_ctx):
    tm, d_model = h_ref.shape
    x = h_ref[...]
    y = x * lax.rsqrt(jnp.mean(x * x, axis=-1, keepdims=True) + EPS) * g_ref[...]
    is_ctx = _row_is_ctx(pl.program_id(0), tm, t_ctx)
    shift = _mod_rows(mod_ref, k_shift, d_model, is_ctx)
    scale = _mod_rows(mod_ref, k_scale, d_model, is_ctx)
    o_ref[...] = (y * (1.0 + scale) + shift).astype(o_ref.dtype)


def _norm_mod(h, g, mod, layer, k_shift, k_scale, t_ctx, out_dtype):
    t_all, d_model = h.shape
    tm = 256
    return pl.pallas_call(
        functools.partial(_norm_mod_kernel, k_shift=k_shift, k_scale=k_scale, t_ctx=t_ctx),
        grid=(t_all // tm,),
        in_specs=[pl.BlockSpec((tm, d_model), lambda i: (i, 0)),
                  pl.BlockSpec((1, d_model), lambda i: (0, 0)),
                  pl.BlockSpec((None, SUBLANES, N_MOD * d_model), lambda i: (layer, 0, 0))],
        out_specs=pl.BlockSpec((tm, d_model), lambda i: (i, 0)),
        out_shape=jax.ShapeDtypeStruct((t_all, d_model), out_dtype),
        compiler_params=_params(1),
        name="norm_mod",
    )(h, g.reshape(1, d_model), mod)


def _mm_kernel(a_ref, w_ref, o_ref):
    o_ref[...] = jnp.dot(a_ref[...], w_ref[...], preferred_element_type=F32).astype(o_ref.dtype)


def _mm(a, w, out_dtype, tn, name):
    m, k = a.shape
    n = w.shape[1]
    tm = _pick_tile(m, (1280, 1024, 640, 512, 256, 128))
    return pl.pallas_call(
        _mm_kernel,
        grid=(n // tn, m // tm),
        in_specs=[pl.BlockSpec((tm, k), lambda j, i: (i, 0)),
                  pl.BlockSpec((k, tn), lambda j, i: (0, j))],
        out_specs=pl.BlockSpec((tm, tn), lambda j, i: (i, j)),
        out_shape=jax.ShapeDtypeStruct((m, n), out_dtype),
        compiler_params=_params(2),
        name=name,
    )(a, w)


def _sgu_kernel(u_ref, v_ref, g_ref, b_ref, wsp_ref, bsp_ref, o_ref):
    v = _gelu(v_ref[...])
    mu = jnp.mean(v, axis=-1, keepdims=True)
    var = jnp.mean(jnp.square(v - mu), axis=-1, keepdims=True)
    vn = ((v - mu) * lax.rsqrt(var + EPS)) * g_ref[...] + b_ref[...]
    vb = vn.astype(BF16)
    gd = v.shape[-1] // A_GROUPS
    for g in range(A_GROUPS):
        cols = slice(g * gd, (g + 1) * gd)
        s = jnp.dot(wsp_ref[g], vb[:, cols], preferred_element_type=F32) + bsp_ref[:, g:g + 1]
        o_ref[:, cols] = (_gelu(u_ref[:, cols]) * s).astype(o_ref.dtype)


def _sgu(proj, ln_g, ln_b, w_sp, b_sp, a_width):
    t_all = proj.shape[0]
    return pl.pallas_call(
        _sgu_kernel,
        grid=(t_all // CHUNK,),
        in_specs=[pl.BlockSpec((CHUNK, a_width), lambda c: (c, 0)),
                  pl.BlockSpec((CHUNK, a_width), lambda c: (c, 1)),
                  pl.BlockSpec((1, a_width), lambda c: (0, 0)),
                  pl.BlockSpec((1, a_width), lambda c: (0, 0)),
                  pl.BlockSpec((A_GROUPS, CHUNK, CHUNK), lambda c: (0, 0, 0)),
                  pl.BlockSpec((CHUNK, A_GROUPS), lambda c: (0, 0))],
        out_specs=pl.BlockSpec((CHUNK, a_width), lambda c: (c, 0)),
        out_shape=jax.ShapeDtypeStruct((t_all, a_width), BF16),
        compiler_params=_params(1),
        name="sgu",
    )(proj, proj, ln_g.reshape(1, a_width), ln_b.reshape(1, a_width), w_sp.astype(BF16), b_sp.T)


def _ssd_chunk_of(i, direction, n_ctx_chunks, n_chunks):
    if direction == 0:
        return i
    return jnp.where(i < n_ctx_chunks, n_ctx_chunks - 1 - i, n_chunks - 1 - (i - n_ctx_chunks))


def _ssd_kernel(xbc_ref, prev_ref, next_ref, dt_ref, cw_ref, cb_ref, dtb_ref, alog_ref, dsk_ref,
                o_ref, state_ref, *, direction, n_ctx_chunks, n_chunks):
    L = CHUNK
    i = pl.program_id(0)
    c = _ssd_chunk_of(i, direction, n_ctx_chunks, n_chunks)
    is_first = jnp.logical_or(c == 0, c == n_ctx_chunks)
    is_last = jnp.logical_or(c == n_ctx_chunks - 1, c == n_chunks - 1)

    @pl.when(i == 0)
    def _():
        state_ref[...] = jnp.zeros_like(state_ref)

    row = lax.broadcasted_iota(jnp.int32, (L, L), 0)
    col = lax.broadcasted_iota(jnp.int32, (L, L), 1)
    causal = (col <= row) if direction == 0 else (col >= row)
    row1 = lax.broadcasted_iota(jnp.int32, (L, 1), 0)

    dt = _softplus(dt_ref[...] + dtb_ref[...])
    a = -jnp.exp(alog_ref[...])
    acs = jnp.dot(causal.astype(F32), dt * a, precision=HIGHEST, preferred_element_type=F32)
    acs_end = acs[L - 1:L, :] if direction == 0 else acs[0:1, :]
    m_all = dt * jnp.exp(acs_end - acs)
    e_in = jnp.exp(acs)
    chunk_decay = jnp.exp(acs_end)
    acs_t = acs.T
    dt_t = dt.T

    def conv_silu(c0, w):
        x = xbc_ref[:, c0:c0 + w]
        pr = jnp.where(is_first, 0.0, prev_ref[SUBLANES - 1:SUBLANES, c0:c0 + w])
        nx = jnp.where(is_last, 0.0, next_ref[0:1, c0:c0 + w])
        xm = jnp.where(row1 == 0, pr, pltpu.roll(x, 1, 0))
        xp = jnp.where(row1 == L - 1, nx, pltpu.roll(x, L - 1, 0))
        y = (xm * cw_ref[0:1, c0:c0 + w] + x * cw_ref[1:2, c0:c0 + w]
             + xp * cw_ref[2:3, c0:c0 + w] + cb_ref[:, c0:c0 + w])
        return _silu(y)

    inner = SSM_HEADS * SSM_HEAD_DIM
    bc = SSM_GROUPS * SSM_STATE
    pair_w = 2 * SSM_HEAD_DIM
    lane = lax.broadcasted_iota(jnp.int32, (L, pair_w), 1)
    lo_half = lane < SSM_HEAD_DIM
    prow = lax.broadcasted_iota(jnp.int32, (pair_w, 1), 0)
    lo_rows = prow < SSM_HEAD_DIM

    for g in range(SSM_GROUPS):
        b_g = conv_silu(inner + g * SSM_STATE, SSM_STATE)
        c_g = conv_silu(inner + bc + g * SSM_STATE, SSM_STATE)
        c_gb = c_g.astype(BF16)
        b_gb = b_g.astype(BF16)
        cb = lax.dot_general(c_gb, b_gb, (((1,), (1,)), ((), ())), preferred_element_type=F32)
        for jj in range(2):
            j = 2 * g + jj
            c0 = direction * SSM_HEADS + 2 * j
            x2f = conv_silu(j * pair_w, pair_w)
            x2 = x2f.astype(BF16)
            zero = jnp.zeros_like(x2)
            xd = jnp.concatenate([jnp.where(lo_half, x2, zero), jnp.where(lo_half, zero, x2)], axis=0)
            l_parts = []
            for cc in (c0, c0 + 1):
                seg = acs[:, cc:cc + 1] - acs_t[cc:cc + 1, :]
                dec = jnp.exp(jnp.where(causal, seg, -jnp.inf))
                l_parts.append((cb * dec * dt_t[cc:cc + 1, :]).astype(BF16))
            l2 = jnp.concatenate(l_parts, axis=1)
            e2 = jnp.where(lo_half, e_in[:, c0:c0 + 1], e_in[:, c0 + 1:c0 + 2])
            m2 = jnp.where(lo_half, m_all[:, c0:c0 + 1], m_all[:, c0 + 1:c0 + 2])
            sp = state_ref[j]
            y = jnp.dot(l2, xd, preferred_element_type=F32)
            y = y + e2 * lax.dot_general(c_gb, sp.astype(BF16), (((1,), (1,)), ((), ())),
                                         preferred_element_type=F32)
            y = y + dsk_ref[:, j * pair_w:(j + 1) * pair_w] * x2f
            upd = lax.dot_general((x2f * m2).astype(BF16), b_gb, (((0,), (0,)), ((), ())),
                                  preferred_element_type=F32)
            cd = jnp.where(lo_rows, chunk_decay[:, c0:c0 + 1], chunk_decay[:, c0 + 1:c0 + 2])
            state_ref[j] = cd * sp + upd
            o_ref[:, j * pair_w:(j + 1) * pair_w] = y


def _ssd(proj, xbc_blk0, dt_raw, conv_w, conv_b, dt_bias_p, a_log_p, d_skip_x, direction, n_ctx_chunks):
    t_all = proj.shape[0]
    n_chunks = t_all // CHUNK
    inner = SSM_HEADS * SSM_HEAD_DIM
    conv_dim = conv_w.shape[1]
    chunk_of = functools.partial(_ssd_chunk_of, direction=direction, n_ctx_chunks=n_ctx_chunks,
                                 n_chunks=n_chunks)
    rows8 = CHUNK // SUBLANES
    last8 = t_all // SUBLANES - 1
    in_specs = [
        pl.BlockSpec((CHUNK, conv_dim), lambda i: (chunk_of(i), xbc_blk0)),
        pl.BlockSpec((SUBLANES, conv_dim), lambda i: (jnp.maximum(chunk_of(i) * rows8 - 1, 0), xbc_blk0)),
        pl.BlockSpec((SUBLANES, conv_dim), lambda i: (jnp.minimum(chunk_of(i) * rows8 + rows8, last8), xbc_blk0)),
        pl.BlockSpec((CHUNK, 128), lambda i: (chunk_of(i), 0)),
        pl.BlockSpec((3, conv_dim), lambda i: (0, 0)),
        pl.BlockSpec((1, conv_dim), lambda i: (0, 0)),
        pl.BlockSpec((1, 128), lambda i: (0, 0)),
        pl.BlockSpec((1, 128), lambda i: (0, 0)),
        pl.BlockSpec((1, inner), lambda i: (0, 0)),
    ]
    return pl.pallas_call(
        functools.partial(_ssd_kernel, direction=direction, n_ctx_chunks=n_ctx_chunks, n_chunks=n_chunks),
        grid=(n_chunks,),
        in_specs=in_specs,
        out_specs=pl.BlockSpec((CHUNK, inner), lambda i: (chunk_of(i), 0)),
        out_shape=jax.ShapeDtypeStruct((t_all, inner), F32),
        scratch_shapes=[pltpu.VMEM((SSM_HEADS // 2, 2 * SSM_HEAD_DIM, SSM_STATE), F32)],
        compiler_params=_params(1),
        name="ssd_bwd" if direction else "ssd_fwd",
    )(proj, proj, proj, dt_raw, conv_w, conv_b.reshape(1, conv_dim), dt_bias_p, a_log_p, d_skip_x)


def _gate_norm_kernel(yf_ref, yb_ref, z_ref, g_ref, o_ref):
    t = (yf_ref[...] + yb_ref[...]) * _silu(z_ref[...])
    scale = lax.rsqrt(jnp.mean(t * t, axis=-1, keepdims=True) + EPS)
    o_ref[...] = (t * scale * g_ref[...]).astype(o_ref.dtype)


def _gate_norm(y_f, y_b, proj, z_blk0, g_ssm):
    t_all, inner = y_f.shape
    tm = 256
    row = pl.BlockSpec((tm, inner), lambda i: (i, 0))
    return pl.pallas_call(
        _gate_norm_kernel,
        grid=(t_all // tm,),
        in_specs=[row, row, pl.BlockSpec((tm, inner), lambda i: (i, z_blk0)),
                  pl.BlockSpec((1, inner), lambda i: (0, 0))],
        out_specs=row,
        out_shape=jax.ShapeDtypeStruct((t_all, inner), BF16),
        compiler_params=_params(1),
        name="gate_norm",
    )(y_f, y_b, proj, g_ssm.reshape(1, inner))


def _dft_tables(n):
    k = np.arange(n, dtype=np.int64)
    ang = 2.0 * np.pi * ((k[:, None] * k[None, :]) % n).astype(np.float64) / n
    return np.cos(ang), np.sin(ang)


def _chan_dft_kernel(a_ref, w_ref, o_ref):
    o_ref[...] = jnp.dot(a_ref[...], w_ref[...], preferred_element_type=F32).astype(o_ref.dtype)


def _chan_dft(f, gd):
    rows, width = f.shape
    groups = width // gd
    cc, sc = _dft_tables(gd)
    w = jnp.asarray(np.concatenate([cc, -sc], axis=1), BF16)
    tm = _pick_tile(rows, (1024, 512, 256, 128))
    return pl.pallas_call(
        _chan_dft_kernel,
        grid=(rows // tm, groups),
        in_specs=[pl.BlockSpec((tm, gd), lambda i, g: (i, g)),
                  pl.BlockSpec((gd, 2 * gd), lambda i, g: (0, 0))],
        out_specs=pl.BlockSpec((tm, 2 * gd), lambda i, g: (i, g)),
        out_shape=jax.ShapeDtypeStruct((rows, 2 * width), BF16),
        compiler_params=_params(2),
        name="fnet_chan",
    )(f, w)


def _pos_dft1_kernel(y_ref, w_ref, tc_ref, ts_ref, o_ref, *, groups, gd):
    n1 = y_ref.shape[0]
    p = jnp.dot(w_ref[...], y_ref[...], preferred_element_type=F32)
    tc = tc_ref[...]
    ts = ts_ref[...]
    for g in range(groups):
        re = slice(g * 2 * gd, g * 2 * gd + gd)
        im = slice(g * 2 * gd + gd, (g + 1) * 2 * gd)
        zr = p[0:n1, re] - p[n1:2 * n1, im]
        zi = p[0:n1, im] + p[n1:2 * n1, re]
        o_ref[:, re] = (zr * tc + zi * ts).astype(o_ref.dtype)
        o_ref[:, im] = (zi * tc - zr * ts).astype(o_ref.dtype)


def _pos_dft2_kernel(z_ref, w_ref, o_ref, *, groups, gd, scale):
    for g in range(groups):
        re = slice(g * 2 * gd, g * 2 * gd + gd)
        im = slice(g * 2 * gd + gd, (g + 1) * 2 * gd)
        zz = jnp.concatenate([z_ref[:, re], z_ref[:, im]], axis=0)
        out = jnp.dot(w_ref[...], zz, preferred_element_type=F32)
        o_ref[:, g * gd:(g + 1) * gd] = (out * scale).astype(o_ref.dtype)


def _pos_dft2(z, n2, groups, gd, scale):
    rows = z.shape[0]
    c2, s2 = _dft_tables(n2)
    w2 = jnp.asarray(np.concatenate([c2, s2], axis=1), BF16)
    return pl.pallas_call(
        functools.partial(_pos_dft2_kernel, groups=groups, gd=gd, scale=np.float32(scale)),
        grid=(rows // n2,),
        in_specs=[pl.BlockSpec((n2, groups * 2 * gd), lambda i: (i, 0)),
                  pl.BlockSpec((n2, 2 * n2), lambda i: (0, 0))],
        out_specs=pl.BlockSpec((n2, groups * gd), lambda i: (i, 0)),
        out_shape=jax.ShapeDtypeStruct((rows, groups * gd), BF16),
        compiler_params=_params(1),
        name="fnet_pos2",
    )(z, w2)


def _fourier_long(f, gd):
    length, width = f.shape
    groups = width // gd
    n2 = CHUNK
    n1 = length // n2
    cw = groups * 2 * gd
    y = _chan_dft(f, gd)
    yt = y.reshape(n1, n2, cw).transpose(1, 0, 2).reshape(length, cw)
    c1, s1 = _dft_tables(n1)
    w1 = jnp.asarray(np.concatenate([c1, -s1], axis=0), BF16)
    ang = 2.0 * np.pi * (np.arange(n2)[:, None] * np.arange(n1)[None, :]).astype(np.float64) / length
    tc = jnp.asarray(np.cos(ang)[:, :, None], F32)
    ts = jnp.asarray(np.sin(ang)[:, :, None], F32)
    zt = pl.pallas_call(
        functools.partial(_pos_dft1_kernel, groups=groups, gd=gd),
        grid=(n2,),
        in_specs=[pl.BlockSpec((n1, cw), lambda i: (i, 0)),
                  pl.BlockSpec((2 * n1, n1), lambda i: (0, 0)),
                  pl.BlockSpec((None, n1, 1), lambda i: (i, 0, 0)),
                  pl.BlockSpec((None, n1, 1), lambda i: (i, 0, 0))],
        out_specs=pl.BlockSpec((n1, cw), lambda i: (i, 0)),
        out_shape=jax.ShapeDtypeStruct((length, cw), BF16),
        compiler_params=_params(1),
        name="fnet_pos1",
    )(yt, w1, tc, ts)
    z = zt.reshape(n2, n1, cw).transpose(1, 0, 2).reshape(length, cw)
    o = _pos_dft2(z, n2, groups, gd, 1.0 / np.sqrt(float(length) * gd))
    return o.reshape(n1, n2, width).transpose(1, 0, 2).reshape(length, width)


def _fourier_short(f, gd):
    length, width = f.shape
    groups = width // gd
    y = _chan_dft(f, gd)
    return _pos_dft2(y, length, groups, gd, 1.0 / np.sqrt(float(length) * gd))


def _merge_kernel(ya_ref, yb_ref, yc_ref, g0_ref, g1_ref, g2_ref, bg_ref, wa_ref, wb_ref, wc_ref, o_ref):
    tn = o_ref.shape[1]
    acc = jax.nn.sigmoid(g0_ref[...] + bg_ref[0:1, :]) * jnp.dot(ya_ref[...], wa_ref[...], preferred_element_type=F32)
    acc = acc + jax.nn.sigmoid(g1_ref[...] + bg_ref[1:2, :]) * jnp.dot(yb_ref[...], wb_ref[...], preferred_element_type=F32)
    acc = acc + jax.nn.sigmoid(g2_ref[...] + bg_ref[2:3, :]) * jnp.dot(yc_ref[...], wc_ref[...], preferred_element_type=F32)
    o_ref[...] = acc.astype(o_ref.dtype)


def _merge(ya, yb, yc, proj, gl_col0, b_gate, w_pa, w_pb, w_pc):
    t_all = ya.shape[0]
    d_model = w_pa.shape[1]
    tn = 512
    tm = _pick_tile(t_all, (640, 512, 256, 128))
    gblk0 = gl_col0 // tn
    nblk = d_model // tn
    return pl.pallas_call(
        _merge_kernel,
        grid=(d_model // tn, t_all // tm),
        in_specs=[pl.BlockSpec((tm, ya.shape[1]), lambda j, i: (i, 0)),
                  pl.BlockSpec((tm, yb.shape[1]), lambda j, i: (i, 0)),
                  pl.BlockSpec((tm, yc.shape[1]), lambda j, i: (i, 0)),
                  pl.BlockSpec((tm, tn), lambda j, i: (i, gblk0 + j)),
                  pl.BlockSpec((tm, tn), lambda j, i: (i, gblk0 + nblk + j)),
                  pl.BlockSpec((tm, tn), lambda j, i: (i, gblk0 + 2 * nblk + j)),
                  pl.BlockSpec((N_BRANCH, tn), lambda j, i: (0, j)),
                  pl.BlockSpec((w_pa.shape[0], tn), lambda j, i: (0, j)),
                  pl.BlockSpec((w_pb.shape[0], tn), lambda j, i: (0, j)),
                  pl.BlockSpec((w_pc.shape[0], tn), lambda j, i: (0, j))],
        out_specs=pl.BlockSpec((tm, tn), lambda j, i: (i, j)),
        out_shape=jax.ShapeDtypeStruct((t_all, d_model), BF16),
        compiler_params=_params(2),
        name="merge",
    )(ya, yb, yc, proj, proj, proj, b_gate.reshape(N_BRANCH, d_model),
      w_pa.astype(BF16), w_pb.astype(BF16), w_pc.astype(BF16))


def _proj_res_kernel(a_ref, w_ref, h_ref, mod_ref, o_ref, *, k_gate, t_ctx, d_model):
    tm, tn = o_ref.shape
    j = pl.program_id(0)
    is_ctx = _row_is_ctx(pl.program_id(1), tm, t_ctx)
    acc = jnp.dot(a_ref[...], w_ref[...], preferred_element_type=F32)
    gate = jnp.where(is_ctx, mod_ref[1:2, :], mod_ref[0:1, :])
    o_ref[...] = h_ref[...] + gate * acc


def _proj_res(a, w, h, mod, layer, k_gate, t_ctx):
    t_all, d_model = h.shape
    tn = 512
    tm = _pick_tile(t_all, (640, 512, 256, 128))
    nblk = d_model // tn
    return pl.pallas_call(
        functools.partial(_proj_res_kernel, k_gate=k_gate, t_ctx=t_ctx, d_model=d_model),
        grid=(d_model // tn, t_all // tm),
        in_specs=[pl.BlockSpec((tm, a.shape[1]), lambda j, i: (i, 0)),
                  pl.BlockSpec((a.shape[1], tn), lambda j, i: (0, j)),
                  pl.BlockSpec((tm, tn), lambda j, i: (i, j)),
                  pl.BlockSpec((None, SUBLANES, tn), lambda j, i: (layer, 0, k_gate * nblk + j))],
        out_specs=pl.BlockSpec((tm, tn), lambda j, i: (i, j)),
        out_shape=jax.ShapeDtypeStruct((t_all, d_model), F32),
        compiler_params=_params(2),
        name="proj_res",
    )(a, w.astype(BF16), h, mod)


ROUTE_EID0, ROUTE_EID1, ROUTE_GATE0, ROUTE_GATE1 = 0, 1, 2, 3


def _ffn_norm_kernel(h_ref, g_ref, mod_ref, wr_ref, br_ref, x_ref, rt_ref, *, t_ctx):
    tm, d_model = h_ref.shape
    x = h_ref[...]
    y = x * lax.rsqrt(jnp.mean(x * x, axis=-1, keepdims=True) + EPS) * g_ref[...]
    is_ctx = _row_is_ctx(pl.program_id(0), tm, t_ctx)
    shift = _mod_rows(mod_ref, 3, d_model, is_ctx)
    scale = _mod_rows(mod_ref, 4, d_model, is_ctx)
    xn = y * (1.0 + scale) + shift
    x_ref[...] = xn
    logits = jnp.dot(xn, wr_ref[...], precision=HIGHEST, preferred_element_type=F32) + br_ref[...]
    lane = lax.broadcasted_iota(jnp.int32, logits.shape, 1).astype(F32)
    far = np.float32(ROUTER_COLS)
    neg = np.float32(-np.inf)
    lg = jnp.where(lane < MOE_GROUPS, logits, neg)
    m = jnp.max(lg, axis=-1, keepdims=True)
    g_top = jnp.min(jnp.where(lg == m, lane, far), axis=-1, keepdims=True)
    p_sel = 1.0 / jnp.sum(jnp.exp(lg - m), axis=-1, keepdims=True)
    lo = MOE_GROUPS + g_top * MOE_EPG
    ls = jnp.where(jnp.logical_and(lane >= lo, lane < lo + MOE_EPG), logits, neg)
    v1 = jnp.max(ls, axis=-1, keepdims=True)
    i1 = jnp.min(jnp.where(ls == v1, lane, far), axis=-1, keepdims=True)
    ls2 = jnp.where(lane == i1, neg, ls)
    v2 = jnp.max(ls2, axis=-1, keepdims=True)
    i2 = jnp.min(jnp.where(ls2 == v2, lane, far), axis=-1, keepdims=True)
    e2 = jnp.exp(v2 - v1)
    den = 1.0 / (1.0 + e2)
    rec = jnp.where(lane == ROUTE_EID0, i1 - MOE_GROUPS, 0.0)
    rec = jnp.where(lane == ROUTE_EID1, i2 - MOE_GROUPS, rec)
    rec = jnp.where(lane == ROUTE_GATE0, p_sel * den, rec)
    rec = jnp.where(lane == ROUTE_GATE1, p_sel * (e2 * den), rec)
    rt_ref[...] = rec


def _ffn_norm(h, g, mod, layer, w_router, b_router, t_ctx):
    t_all, d_model = h.shape
    tm = 256
    return pl.pallas_call(
        functools.partial(_ffn_norm_kernel, t_ctx=t_ctx),
        grid=(t_all // tm,),
        in_specs=[pl.BlockSpec((tm, d_model), lambda i: (i, 0)),
                  pl.BlockSpec((1, d_model), lambda i: (0, 0)),
                  pl.BlockSpec((None, SUBLANES, N_MOD * d_model), lambda i: (layer, 0, 0)),
                  pl.BlockSpec((d_model, ROUTER_COLS), lambda i: (0, 0)),
                  pl.BlockSpec((1, ROUTER_COLS), lambda i: (0, 0))],
        out_specs=[pl.BlockSpec((tm, d_model), lambda i: (i, 0)),
                   pl.BlockSpec((tm, ROUTER_COLS), lambda i: (i, 0))],
        out_shape=[jax.ShapeDtypeStruct((t_all, d_model), F32),
                   jax.ShapeDtypeStruct((t_all, ROUTER_COLS), F32)],
        compiler_params=_params(1),
        name="ffn_norm",
    )(h, g.reshape(1, d_model), mod, w_router, b_router)


def _moe_kernel(be_ref, nact_ref, first_ref, wslot_ref, nexte_ref,
                tok_ref, tok_next_ref, dst_ref, x_hbm, w1_hbm, w3_hbm, w2_hbm, bw_ref,
                y_hbm, xbuf0, xbuf1, ybuf0, ybuf1, wf1, wf3, wf2, w1b, w3b, w2b, sem_in, sem_out, sem_w,
                *, layer):
    b = pl.program_id(0)
    n_active = nact_ref[0]
    xbufs = (xbuf0, xbuf1)
    ybufs = (ybuf0, ybuf1)

    def weight_copies(e, ws):
        return [pltpu.make_async_copy(w1_hbm.at[layer, e], wf1.at[ws], sem_w.at[ws]),
                pltpu.make_async_copy(w3_hbm.at[layer, e], wf3.at[ws], sem_w.at[ws]),
                pltpu.make_async_copy(w2_hbm.at[layer, e], wf2.at[ws], sem_w.at[ws])]

    def gather_copy(row_index, r, s, sem_slot):
        return pltpu.make_async_copy(x_hbm.at[pl.ds(row_index, 1), :], xbufs[s].at[pl.ds(r, 1), :],
                                     sem_in.at[sem_slot])

    def scatter_copy(row_index, r, s, sem_slot):
        return pltpu.make_async_copy(ybufs[s].at[pl.ds(r, 1), :], y_hbm.at[pl.ds(row_index, 1), :],
                                     sem_out.at[sem_slot])

    def wait_rows(copy):
        for _ in range(MOE_BLOCK):
            copy.wait()

    @pl.when(b == 0)
    def _():
        for cp in weight_copies(be_ref[0], 0):
            cp.start()
        for r in range(MOE_BLOCK):
            gather_copy(tok_ref[0, r], r, 0, 0).start()

    def step(s):
        @pl.when(first_ref[b] == 1)
        def _():
            ws = wslot_ref[b]
            for cp in weight_copies(be_ref[b], ws):
                cp.wait()
            w1b[...] = wf1[ws].astype(BF16)
            w3b[...] = wf3[ws].astype(BF16)
            w2b[...] = wf2[ws].astype(BF16)
            nxt = nexte_ref[b]

            @pl.when(nxt >= 0)
            def _():
                for cp in weight_copies(nxt, 1 - ws):
                    cp.start()

        wait_rows(gather_copy(0, 0, s, s))

        @pl.when(b >= 2)
        def _():
            wait_rows(scatter_copy(0, 0, s, s))

        for r in range(MOE_BLOCK):
            gather_copy(tok_next_ref[0, r], r, 1 - s, 1 - s).start()
        x = xbufs[s][...].astype(BF16)
        h1 = jnp.dot(x, w1b[...], preferred_element_type=F32)
        h3 = jnp.dot(x, w3b[...], preferred_element_type=F32)
        hh = (_silu(h1) * h3).astype(BF16)
        ybufs[s][...] = jnp.dot(hh, w2b[...], preferred_element_type=F32) * bw_ref[...]
        for r in range(MOE_BLOCK):
            scatter_copy(dst_ref[0, r], r, s, s).start()

    active = b < n_active
    for s in range(2):
        pl.when(jnp.logical_and(active, b % 2 == s))(functools.partial(step, s))

    @pl.when(b == n_active - 1)
    def _():
        sb = b % 2
        wait_rows(gather_copy(0, 0, 0, 1 - sb))

        @pl.when(b >= 1)
        def _():
            wait_rows(scatter_copy(0, 0, 0, 1 - sb))
        wait_rows(scatter_copy(0, 0, 0, sb))
        ybuf0[...] = jnp.zeros(ybuf0.shape, F32)
        spare0 = y_hbm.shape[0] - 2 * MOE_BLOCK
        fills = [pltpu.make_async_copy(ybuf0, y_hbm.at[pl.ds(spare0 + k * MOE_BLOCK, MOE_BLOCK), :], sem_out.at[0])
                 for k in range(2)]
        for cp in fills:
            cp.start()
        for cp in fills:
            cp.wait()


def _moe_blocks(layer, tables, xn, w1, w3, w2, n_out_rows):
    block_e, n_active, first, wslot, next_e, tok_idx, dst_idx, buf_w = tables
    t_all, d_model = xn.shape
    ff = w1.shape[3]
    n_blocks = tok_idx.shape[0]
    idx_spec = functools.partial(pl.BlockSpec, (None, 1, MOE_BLOCK), memory_space=pltpu.SMEM)
    grid_spec = pltpu.PrefetchScalarGridSpec(
        num_scalar_prefetch=5,
        grid=(n_blocks,),
        in_specs=[idx_spec(index_map=lambda b, *_: (b, 0, 0)),
                  idx_spec(index_map=lambda b, *_: (jnp.minimum(b + 1, n_blocks - 1), 0, 0)),
                  idx_spec(index_map=lambda b, *_: (b, 0, 0)),
                  pl.BlockSpec(memory_space=pl.ANY),
                  pl.BlockSpec(memory_space=pl.ANY),
                  pl.BlockSpec(memory_space=pl.ANY),
                  pl.BlockSpec(memory_space=pl.ANY),
                  pl.BlockSpec((MOE_BLOCK, 1), lambda b, *_: (b, 0))],
        out_specs=pl.BlockSpec(memory_space=pl.ANY),
        scratch_shapes=[pltpu.VMEM((MOE_BLOCK, d_model), F32), pltpu.VMEM((MOE_BLOCK, d_model), F32),
                        pltpu.VMEM((MOE_BLOCK, d_model), F32), pltpu.VMEM((MOE_BLOCK, d_model), F32),
                        pltpu.VMEM((2, d_model, ff), F32), pltpu.VMEM((2, d_model, ff), F32),
                        pltpu.VMEM((2, ff, d_model), F32),
                        pltpu.VMEM((d_model, ff), BF16), pltpu.VMEM((d_model, ff), BF16),
                        pltpu.VMEM((ff, d_model), BF16),
                        pltpu.SemaphoreType.DMA((2,)), pltpu.SemaphoreType.DMA((2,)),
                        pltpu.SemaphoreType.DMA((2,))],
    )
    return pl.pallas_call(
        functools.partial(_moe_kernel, layer=layer),
        grid_spec=grid_spec,
        out_shape=jax.ShapeDtypeStruct((n_out_rows, d_model), F32),
        compiler_params=_params(1),
        name="moe_blocks",
    )(block_e, n_active, first, wslot, next_e, tok_idx, tok_idx, dst_idx, xn, w1, w3, w2, buf_w.reshape(-1, 1))


def _dispatch(route, t_all):
    eid = route[:, ROUTE_EID0:ROUTE_EID1 + 1].astype(jnp.int32)
    gate = route[:, ROUTE_GATE0:ROUTE_GATE1 + 1]
    n_asg = t_all * MOE_TOPK
    flat_e = eid.reshape(n_asg)
    flat_w = gate.reshape(n_asg)
    order = jnp.argsort(flat_e).astype(jnp.int32)
    experts = jnp.arange(MOE_EXPERTS, dtype=jnp.int32)
    counts = jnp.sum((flat_e[:, None] == experts[None, :]).astype(jnp.int32), axis=0)
    padded = (counts + MOE_BLOCK - 1) // MOE_BLOCK * MOE_BLOCK
    pad_end = jnp.cumsum(padded)
    pad_start = pad_end - padded
    start = jnp.cumsum(counts) - counts
    n_blocks = (n_asg + MOE_EXPERTS * (MOE_BLOCK - 1)) // MOE_BLOCK + 1
    blk = jnp.arange(n_blocks, dtype=jnp.int32)
    blk_e = jnp.minimum(jnp.sum((pad_end[None, :] <= (blk * MOE_BLOCK)[:, None]).astype(jnp.int32), axis=1),
                        MOE_EXPERTS - 1)
    n_active = pad_end[-1] // MOE_BLOCK
    is_act = blk < n_active
    block_e = jnp.where(is_act, blk_e, 0)
    prev_e = jnp.concatenate([jnp.full((1,), -1, jnp.int32), blk_e[:-1]])
    first = jnp.logical_and(is_act, blk_e != prev_e).astype(jnp.int32)
    has_tok = counts > 0
    rank = jnp.cumsum(has_tok.astype(jnp.int32)) - 1
    wslot = rank[blk_e] % 2
    later = lax.cummin(jnp.where(has_tok, experts, MOE_EXPERTS)[::-1])[::-1]
    nxt = jnp.concatenate([later[1:], jnp.full((1,), MOE_EXPERTS, jnp.int32)])
    next_e = jnp.where(nxt >= MOE_EXPERTS, -1, nxt)[blk_e]
    r_in_blk = jnp.arange(MOE_BLOCK, dtype=jnp.int32)[None, :]
    r = (blk * MOE_BLOCK - pad_start[blk_e])[:, None] + r_in_blk
    valid = jnp.logical_and(r < counts[blk_e][:, None], (blk < n_active)[:, None])
    p = jnp.clip(start[blk_e][:, None] + r, 0, n_asg - 1)
    a = jnp.take(order, p, mode="clip")
    tok = jnp.where(valid, a >> 1, 0)
    spare = MOE_TOPK * t_all + (blk % 2)[:, None] * MOE_BLOCK + r_in_blk
    dst = jnp.where(valid, (a & 1) * t_all + (a >> 1), spare)
    w = jnp.where(valid, jnp.take(flat_w, a, mode="clip"), 0.0)
    return (block_e.astype(jnp.int32), n_active.astype(jnp.int32).reshape(1),
            first, wslot.astype(jnp.int32), next_e.astype(jnp.int32),
            tok.reshape(n_blocks, 1, MOE_BLOCK).astype(jnp.int32),
            dst.reshape(n_blocks, 1, MOE_BLOCK).astype(jnp.int32), w.reshape(-1))


def _moe_res_norm_kernel(h_ref, y0_ref, y1_ref, mod_ref, g_ref, *rest, t_ctx, tile0, last):
    tm, d_model = h_ref.shape
    is_ctx = _row_is_ctx(pl.program_id(0) + tile0, tm, t_ctx)
    h = h_ref[...] + _mod_rows(mod_ref, 5, d_model, is_ctx) * (y0_ref[...] + y1_ref[...])
    y = h * lax.rsqrt(jnp.mean(h * h, axis=-1, keepdims=True) + EPS) * g_ref[...]
    if last:
        (o_ref,) = rest
        o_ref[...] = y
    else:
        nmod_ref, ho_ref, hn_ref = rest
        ho_ref[...] = h
        shift = _mod_rows(nmod_ref, 0, d_model, is_ctx)
        scale = _mod_rows(nmod_ref, 1, d_model, is_ctx)
        hn_ref[...] = (y * (1.0 + scale) + shift).astype(hn_ref.dtype)


def _moe_res_norm(h, y01, mod, layer, g_next, t_ctx, last):
    t_all, d_model = h.shape
    tm = 256
    nt = t_all // tm
    tile0 = t_ctx // tm if last else 0
    row = pl.BlockSpec((tm, d_model), lambda i: (i + tile0, 0))
    mod_spec = lambda lyr: pl.BlockSpec((None, SUBLANES, N_MOD * d_model), lambda i: (lyr, 0, 0))
    in_specs = [row, row, pl.BlockSpec((tm, d_model), lambda i: (i + tile0 + nt, 0)), mod_spec(layer),
                pl.BlockSpec((1, d_model), lambda i: (0, 0))]
    args = [h, y01, y01, mod, g_next.reshape(1, d_model)]
    out_row = pl.BlockSpec((tm, d_model), lambda i: (i, 0))
    if last:
        out_specs = out_row
        out_shape = jax.ShapeDtypeStruct((t_all - t_ctx, d_model), F32)
    else:
        in_specs.append(mod_spec(layer + 1))
        args.append(mod)
        out_specs = [out_row, out_row]
        out_shape = [jax.ShapeDtypeStruct((t_all, d_model), F32), jax.ShapeDtypeStruct((t_all, d_model), BF16)]
    return pl.pallas_call(
        functools.partial(_moe_res_norm_kernel, t_ctx=t_ctx, tile0=tile0, last=last),
        grid=(nt - tile0,),
        in_specs=in_specs,
        out_specs=out_specs,
        out_shape=out_shape,
        compiler_params=_params(1),
        name="moe_res_norm",
    )(*args)


def kernel(x, c, ctx, c_ctx, w_ada, b_ada, g_mix, w_in, b_gate, ln_a_g, ln_a_b, w_sp, b_sp, conv_w, conv_b, dt_bias, a_log, d_skip, g_ssm, w_pa, w_pb, w_pc, w_out, g_ffn, w_rg, b_rg, w_re, b_re, w_e1, w_e3, w_e2, g_final):
    bsz, t_lat, d_model = x.shape
    assert bsz == 1 and c.shape[0] == 1
    t_ctx = ctx.shape[1]
    depth = w_ada.shape[0]
    a_width = ln_a_g.shape[1]
    inner = SSM_HEADS * SSM_HEAD_DIM
    conv_dim = conv_w.shape[2]
    f_width = w_pc.shape[1]
    f_gd = f_width // F_GROUPS
    n_ctx_chunks = t_ctx // CHUNK
    t_all = t_ctx + t_lat
    assert t_ctx % 256 == 0 and t_lat % 256 == 0 and t_lat % (CHUNK * SUBLANES) == 0

    c_u, c_z = 0, 2 * a_width
    c_xbc = c_z + inner
    c_dt = c_xbc + conv_dim
    c_fc = c_dt + 2 * SSM_HEADS
    c_gl = c_fc + f_width
    n_gl = N_BRANCH * d_model
    p_z, p_xbc, p_gl = 2 * a_width, 2 * a_width + inner, 2 * a_width + inner + conv_dim

    cs = jnp.zeros((SUBLANES, d_model), F32).at[0].set(c[0]).at[1].set(c_ctx)
    mod = _ada(cs, w_ada, b_ada)

    h = jnp.concatenate([ctx[0], x[0]], axis=0)
    pad_heads = jnp.zeros((128 - 2 * SSM_HEADS,), F32)

    for l in range(depth):
        wl = w_in[l]
        w_main = jnp.concatenate([wl[:, c_u:c_dt], wl[:, c_gl:c_gl + n_gl]], axis=1).astype(BF16)
        w_dt = jnp.concatenate([wl[:, c_dt:c_fc], jnp.zeros((d_model, 128 - 2 * SSM_HEADS), F32)], axis=1).astype(BF16)
        w_fc = wl[:, c_fc:c_gl].astype(BF16)

        if l == 0:
            hn = _norm_mod(h, g_mix[l], mod, l, 0, 1, t_ctx, BF16)
        proj = _mm(hn, w_main, F32, 1024, "proj_main")
        dt_raw = _mm(hn, w_dt, F32, 128, "proj_dt")
        fc = _mm(hn, w_fc, BF16, 1024, "proj_fc")

        ya = _sgu(proj, ln_a_g[l], ln_a_b[l], w_sp[l], b_sp[l], a_width)

        dtb = jnp.concatenate([dt_bias[l].reshape(-1), pad_heads]).reshape(1, 128)
        alg = jnp.concatenate([a_log[l].reshape(-1), pad_heads]).reshape(1, 128)
        dsk = jnp.repeat(d_skip[l], SSM_HEAD_DIM, axis=-1)
        assert p_xbc % conv_dim == 0 and p_z % inner == 0
        y_f = _ssd(proj, p_xbc // conv_dim, dt_raw, conv_w[l], conv_b[l], dtb, alg, dsk[0:1], 0, n_ctx_chunks)
        y_b = _ssd(proj, p_xbc // conv_dim, dt_raw, conv_w[l], conv_b[l], dtb, alg, dsk[1:2], 1, n_ctx_chunks)
        yb = _gate_norm(y_f, y_b, proj, p_z // inner, g_ssm[l])

        yc_lat = _fourier_long(fc[t_ctx:], f_gd)
        if l < depth - 1:
            yc_ctx = _fourier_short(fc[:t_ctx], f_gd)
        else:
            yc_ctx = jnp.zeros((t_ctx, f_width), BF16)
        yc = jnp.concatenate([yc_ctx, yc_lat], axis=0)

        merged = _merge(ya, yb, yc, proj, p_gl, b_gate[l], w_pa[l], w_pb[l], w_pc[l])
        h = _proj_res(merged, w_out[l], h, mod, l, 2, t_ctx)

        w_router = jnp.concatenate([w_rg[l], w_re[l], jnp.zeros((d_model, ROUTER_COLS - MOE_GROUPS - MOE_EXPERTS), F32)], axis=1)
        b_router = jnp.concatenate([b_rg[l], b_re[l], jnp.zeros((ROUTER_COLS - MOE_GROUPS - MOE_EXPERTS,), F32)]).reshape(1, ROUTER_COLS)
        xn, route = _ffn_norm(h, g_ffn[l], mod, l, w_router, b_router, t_ctx)
        y01 = _moe_blocks(l, _dispatch(route, t_all), xn, w_e1, w_e3, w_e2, MOE_TOPK * t_all + 2 * MOE_BLOCK)
        if l < depth - 1:
            h, hn = _moe_res_norm(h, y01, mod, l, g_mix[l + 1], t_ctx, False)
        else:
            out = _moe_res_norm(h, y01, mod, l, g_final, t_ctx, True)
    return out.reshape(bsz, t_lat, d_model)
```

```python
import functools

import numpy as np
import jax
import jax.numpy as jnp
from jax import lax
from jax.experimental import pallas as pl
from jax.experimental.pallas import tpu as pltpu

F32 = jnp.float32
BF16 = jnp.bfloat16
HIGHEST = lax.Precision.HIGHEST

EPS = 1e-6
N_MOD = 6
GRID_W = 64
CHUNK = 128
A_GROUPS = 8
SSM_HEADS = 32
SSM_HEAD_DIM = 64
SSM_GROUPS = 8
SSM_STATE = 128
F_GROUPS = 4
N_BRANCH = 3
MOE_GROUPS = 8
MOE_EPG = 8
MOE_EXPERTS = 64
MOE_TOPK = 2
MOE_BLOCK = 128
ROUTER_COLS = 128

V7X_VMEM_LIMIT_BYTES = 56 * 1024 * 1024
SUBLANES = 8
LANES = 128


def _params(n_axes):
    return pltpu.CompilerParams(dimension_semantics=("arbitrary",) * n_axes,
                                vmem_limit_bytes=V7X_VMEM_LIMIT_BYTES)


def _pick_tile(n, candidates):
    for t in candidates:
        if n % t == 0:
            return t
    raise ValueError(f"no tile for {n}")


def _gelu(x):
    return 0.5 * x * (1.0 + lax.erf(x * np.float32(1.0 / np.sqrt(2.0))))


def _silu(x):
    return x * jax.nn.sigmoid(x)


def _softplus(x):
    return jnp.maximum(x, 0.0) + jnp.log(1.0 + jnp.exp(-jnp.abs(x)))


def _row_is_ctx(tile_index, tm, t_ctx):
    rows = tile_index * tm + lax.broadcasted_iota(jnp.int32, (tm, 1), 0)
    return rows < t_ctx


def _mod_rows(mod_ref, k, d_model, is_ctx):
    lo = k * d_model
    lat = mod_ref[0:1, lo:lo + d_model]
    ctx = mod_ref[1:2, lo:lo + d_model]
    return jnp.where(is_ctx, ctx, lat)


def _ada_kernel(cs_ref, w_ref, b_ref, o_ref):
    s = _silu(cs_ref[...])
    o_ref[...] = jnp.dot(s, w_ref[...], precision=HIGHEST, preferred_element_type=F32) + b_ref[...]


def _ada(cs, w_ada, b_ada):
    depth, d_model, n = w_ada.shape
    tn = 1024
    return pl.pallas_call(
        _ada_kernel,
        grid=(depth, n // tn),
        in_specs=[pl.BlockSpec((SUBLANES, d_model), lambda l, j: (0, 0)),
                  pl.BlockSpec((None, d_model, tn), lambda l, j: (l, 0, j)),
                  pl.BlockSpec((None, 1, tn), lambda l, j: (l, 0, j))],
        out_specs=pl.BlockSpec((None, SUBLANES, tn), lambda l, j: (l, 0, j)),
        out_shape=jax.ShapeDtypeStruct((depth, SUBLANES, n), F32),
        compiler_params=_params(2),
        name="ada",
    )(cs, w_ada, b_ada.reshape(depth, 1, n))


def _norm_mod_kernel(h_ref, g_ref, mod_ref, o_ref, *, k_shift, k_scale, t_ctx):
    tm, d_model = h_ref.shape
    x = h_ref[...]
    y = x * lax.rsqrt(jnp.mean(x * x, axis=-1, keepdims=True) + EPS) * g_ref[...]
    is_ctx = _row_is_ctx(pl.program_id(0), tm, t_ctx)
    shift = _mod_rows(mod_ref, k_shift, d_model, is_ctx)
    scale = _mod_rows(mod_ref, k_scale, d_model, is_ctx)
    o_ref[...] = (y * (1.0 + scale) + shift).astype(o_ref.dtype)


def _norm_mod(h, g, mod, layer, k_shift, k_scale, t_ctx, out_dtype):
    t_all, d_model = h.shape
    tm = 256
    return pl.pallas_call(
        functools.partial(_norm_mod_kernel, k_shift=k_shift, k_scale=k_scale, t_ctx=t_ctx),
        grid=(t_all // tm,),
        in_specs=[pl.BlockSpec((tm, d_model), lambda i: (i, 0)),
                  pl.BlockSpec((1, d_model), lambda i: (0, 0)),
                  pl.BlockSpec((None, SUBLANES, N_MOD * d_model), lambda i: (layer, 0, 0))],
        out_specs=pl.BlockSpec((tm, d_model), lambda i: (i, 0)),
        out_shape=jax.ShapeDtypeStruct((t_all, d_model), out_dtype),
        compiler_params=_params(1),
        name="norm_mod",
    )(h, g.reshape(1, d_model), mod)


def _mm_kernel(a_ref, w_ref, o_ref):
    o_ref[...] = jnp.dot(a_ref[...], w_ref[...], preferred_element_type=F32).astype(o_ref.dtype)


def _mm(a, w, out_dtype, tn, name):
    m, k = a.shape
    n = w.shape[1]
    tm = _pick_tile(m, (1280, 1024, 640, 512, 256, 128))
    return pl.pallas_call(
        _mm_kernel,
        grid=(n // tn, m // tm),
        in_specs=[pl.BlockSpec((tm, k), lambda j, i: (i, 0)),
                  pl.BlockSpec((k, tn), lambda j, i: (0, j))],
        out_specs=pl.BlockSpec((tm, tn), lambda j, i: (i, j)),
        out_shape=jax.ShapeDtypeStruct((m, n), out_dtype),
        compiler_params=_params(2),
        name=name,
    )(a, w)


def _mm_gate_kernel(a_ref, w_ref, b_ref, o_ref):
    acc = jnp.dot(a_ref[...], w_ref[...], preferred_element_type=F32)
    o_ref[...] = jax.nn.sigmoid(acc + b_ref[...]).astype(o_ref.dtype)


def _mm_gate(a, w, b, tn):
    m, k = a.shape
    n = w.shape[1]
    tm = _pick_tile(m, (1280, 1024, 640, 512, 256, 128))
    return pl.pallas_call(
        _mm_gate_kernel,
        grid=(n // tn, m // tm),
        in_specs=[pl.BlockSpec((tm, k), lambda j, i: (i, 0)),
                  pl.BlockSpec((k, tn), lambda j, i: (0, j)),
                  pl.BlockSpec((1, tn), lambda j, i: (0, j))],
        out_specs=pl.BlockSpec((tm, tn), lambda j, i: (i, j)),
        out_shape=jax.ShapeDtypeStruct((m, n), BF16),
        compiler_params=_params(2),
        name="proj_gate",
    )(a, w, b.reshape(1, n))


def _sgu_kernel(u_ref, v_ref, g_ref, b_ref, wsp_ref, bsp_ref, o_ref):
    v = _gelu(v_ref[...])
    mu = jnp.mean(v, axis=-1, keepdims=True)
    var = jnp.mean(jnp.square(v - mu), axis=-1, keepdims=True)
    vn = ((v - mu) * lax.rsqrt(var + EPS)) * g_ref[...] + b_ref[...]
    vb = vn.astype(BF16)
    gd = v.shape[-1] // A_GROUPS
    for g in range(A_GROUPS):
        cols = slice(g * gd, (g + 1) * gd)
        s = jnp.dot(wsp_ref[g], vb[:, cols], preferred_element_type=F32) + bsp_ref[:, g:g + 1]
        o_ref[:, cols] = (_gelu(u_ref[:, cols]) * s).astype(o_ref.dtype)


def _sgu(proj, ln_g, ln_b, w_sp, b_sp, a_width):
    t_all = proj.shape[0]
    return pl.pallas_call(
        _sgu_kernel,
        grid=(t_all // CHUNK,),
        in_specs=[pl.BlockSpec((CHUNK, a_width), lambda c: (c, 0)),
                  pl.BlockSpec((CHUNK, a_width), lambda c: (c, 1)),
                  pl.BlockSpec((1, a_width), lambda c: (0, 0)),
                  pl.BlockSpec((1, a_width), lambda c: (0, 0)),
                  pl.BlockSpec((A_GROUPS, CHUNK, CHUNK), lambda c: (0, 0, 0)),
                  pl.BlockSpec((CHUNK, A_GROUPS), lambda c: (0, 0))],
        out_specs=pl.BlockSpec((CHUNK, a_width), lambda c: (c, 0)),
        out_shape=jax.ShapeDtypeStruct((t_all, a_width), BF16),
        compiler_params=_params(1),
        name="sgu",
    )(proj, proj, ln_g.reshape(1, a_width), ln_b.reshape(1, a_width), w_sp.astype(BF16), b_sp.T)


def _ssd_chunk_of(i, direction, n_ctx_chunks, n_chunks):
    if direction == 0:
        return i
    return jnp.where(i < n_ctx_chunks, n_ctx_chunks - 1 - i, n_chunks - 1 - (i - n_ctx_chunks))


def _ssd_kernel(xbc_ref, prev_ref, next_ref, dt_ref, cw_ref, cb_ref, dtb_ref, alog_ref, dsk_ref,
                o_ref, state_ref, *, direction, n_ctx_chunks, n_chunks):
    L = CHUNK
    i = pl.program_id(0)
    c = _ssd_chunk_of(i, direction, n_ctx_chunks, n_chunks)
    is_first = jnp.logical_or(c == 0, c == n_ctx_chunks)
    is_last = jnp.logical_or(c == n_ctx_chunks - 1, c == n_chunks - 1)

    @pl.when(i == 0)
    def _():
        state_ref[...] = jnp.zeros_like(state_ref)

    row = lax.broadcasted_iota(jnp.int32, (L, L), 0)
    col = lax.broadcasted_iota(jnp.int32, (L, L), 1)
    causal = (col <= row) if direction == 0 else (col >= row)
    row1 = lax.broadcasted_iota(jnp.int32, (L, 1), 0)

    dt = _softplus(dt_ref[...] + dtb_ref[...])
    a = -jnp.exp(alog_ref[...])
    tri = jnp.where(causal, 1.0, 0.0).astype(BF16)
    acs = jnp.zeros((L, dt.shape[1]), F32)
    rem = dt * a
    for _ in range(3):
        part = rem.astype(BF16)
        acs = acs + jnp.dot(tri, part, preferred_element_type=F32)
        rem = rem - part.astype(F32)
    acs_end = acs[L - 1:L, :] if direction == 0 else acs[0:1, :]
    m_all = dt * jnp.exp(acs_end - acs)
    e_in = jnp.exp(acs)
    chunk_decay = jnp.exp(acs_end)
    acs_t = acs.T
    dt_t = dt.T

    def conv_silu(c0, w):
        x = xbc_ref[:, c0:c0 + w]
        pr = jnp.where(is_first, 0.0, prev_ref[SUBLANES - 1:SUBLANES, c0:c0 + w])
        nx = jnp.where(is_last, 0.0, next_ref[0:1, c0:c0 + w])
        xm = jnp.where(row1 == 0, pr, pltpu.roll(x, 1, 0))
        xp = jnp.where(row1 == L - 1, nx, pltpu.roll(x, L - 1, 0))
        y = (xm * cw_ref[0:1, c0:c0 + w] + x * cw_ref[1:2, c0:c0 + w]
             + xp * cw_ref[2:3, c0:c0 + w] + cb_ref[:, c0:c0 + w])
        return _silu(y)

    inner = SSM_HEADS * SSM_HEAD_DIM
    bc = SSM_GROUPS * SSM_STATE
    pair_w = 2 * SSM_HEAD_DIM
    lane = lax.broadcasted_iota(jnp.int32, (L, pair_w), 1)
    lo_half = lane < SSM_HEAD_DIM
    prow = lax.broadcasted_iota(jnp.int32, (pair_w, 1), 0)
    lo_rows = prow < SSM_HEAD_DIM

    for g in range(SSM_GROUPS):
        b_g = conv_silu(inner + g * SSM_STATE, SSM_STATE)
        c_g = conv_silu(inner + bc + g * SSM_STATE, SSM_STATE)
        c_gb = c_g.astype(BF16)
        b_gb = b_g.astype(BF16)
        cb = lax.dot_general(c_gb, b_gb, (((1,), (1,)), ((), ())), preferred_element_type=F32)
        for jj in range(2):
            j = 2 * g + jj
            c0 = direction * SSM_HEADS + 2 * j
            x2f = conv_silu(j * pair_w, pair_w)
            x2 = x2f.astype(BF16)
            zero = jnp.zeros_like(x2)
            xd = jnp.concatenate([jnp.where(lo_half, x2, zero), jnp.where(lo_half, zero, x2)], axis=0)
            l_parts = []
            for cc in (c0, c0 + 1):
                seg = acs[:, cc:cc + 1] - acs_t[cc:cc + 1, :]
                dec = jnp.exp(jnp.where(causal, seg, -jnp.inf))
                l_parts.append((cb * dec * dt_t[cc:cc + 1, :]).astype(BF16))
            l2 = jnp.concatenate(l_parts, axis=1)
            e2 = jnp.where(lo_half, e_in[:, c0:c0 + 1], e_in[:, c0 + 1:c0 + 2])
            m2 = jnp.where(lo_half, m_all[:, c0:c0 + 1], m_all[:, c0 + 1:c0 + 2])
            sp = state_ref[j]
            y = jnp.dot(l2, xd, preferred_element_type=F32)
            y = y + e2 * lax.dot_general(c_gb, sp.astype(BF16), (((1,), (1,)), ((), ())),
                                         preferred_element_type=F32)
            y = y + dsk_ref[:, j * pair_w:(j + 1) * pair_w] * x2f
            upd = lax.dot_general((x2f * m2).astype(BF16), b_gb, (((0,), (0,)), ((), ())),
                                  preferred_element_type=F32)
            cd = jnp.where(lo_rows, chunk_decay[:, c0:c0 + 1], chunk_decay[:, c0 + 1:c0 + 2])
            state_ref[j] = cd * sp + upd
            o_ref[:, j * pair_w:(j + 1) * pair_w] = y


def _ssd(proj, xbc_blk0, dt_raw, conv_w, conv_b, dt_bias_p, a_log_p, d_skip_x, direction, n_ctx_chunks):
    t_all = proj.shape[0]
    n_chunks = t_all // CHUNK
    inner = SSM_HEADS * SSM_HEAD_DIM
    conv_dim = conv_w.shape[1]
    chunk_of = functools.partial(_ssd_chunk_of, direction=direction, n_ctx_chunks=n_ctx_chunks,
                                 n_chunks=n_chunks)
    rows8 = CHUNK // SUBLANES
    last8 = t_all // SUBLANES - 1
    in_specs = [
        pl.BlockSpec((CHUNK, conv_dim), lambda i: (chunk_of(i), xbc_blk0)),
        pl.BlockSpec((SUBLANES, conv_dim), lambda i: (jnp.maximum(chunk_of(i) * rows8 - 1, 0), xbc_blk0)),
        pl.BlockSpec((SUBLANES, conv_dim), lambda i: (jnp.minimum(chunk_of(i) * rows8 + rows8, last8), xbc_blk0)),
        pl.BlockSpec((CHUNK, 128), lambda i: (chunk_of(i), 0)),
        pl.BlockSpec((3, conv_dim), lambda i: (0, 0)),
        pl.BlockSpec((1, conv_dim), lambda i: (0, 0)),
        pl.BlockSpec((1, 128), lambda i: (0, 0)),
        pl.BlockSpec((1, 128), lambda i: (0, 0)),
        pl.BlockSpec((1, inner), lambda i: (0, 0)),
    ]
    return pl.pallas_call(
        functools.partial(_ssd_kernel, direction=direction, n_ctx_chunks=n_ctx_chunks, n_chunks=n_chunks),
        grid=(n_chunks,),
        in_specs=in_specs,
        out_specs=pl.BlockSpec((CHUNK, inner), lambda i: (chunk_of(i), 0)),
        out_shape=jax.ShapeDtypeStruct((t_all, inner), F32),
        scratch_shapes=[pltpu.VMEM((SSM_HEADS // 2, 2 * SSM_HEAD_DIM, SSM_STATE), F32)],
        compiler_params=_params(1),
        name="ssd_bwd" if direction else "ssd_fwd",
    )(proj, proj, proj, dt_raw, conv_w, conv_b.reshape(1, conv_dim), dt_bias_p, a_log_p, d_skip_x)


def _gate_norm_kernel(yf_ref, yb_ref, z_ref, g_ref, o_ref):
    t = (yf_ref[...] + yb_ref[...]) * _silu(z_ref[...])
    scale = lax.rsqrt(jnp.mean(t * t, axis=-1, keepdims=True) + EPS)
    o_ref[...] = (t * scale * g_ref[...]).astype(o_ref.dtype)


def _gate_norm(y_f, y_b, proj, z_blk0, g_ssm):
    t_all, inner = y_f.shape
    tm = 256
    row = pl.BlockSpec((tm, inner), lambda i: (i, 0))
    return pl.pallas_call(
        _gate_norm_kernel,
        grid=(t_all // tm,),
        in_specs=[row, row, pl.BlockSpec((tm, inner), lambda i: (i, z_blk0)),
                  pl.BlockSpec((1, inner), lambda i: (0, 0))],
        out_specs=row,
        out_shape=jax.ShapeDtypeStruct((t_all, inner), BF16),
        compiler_params=_params(1),
        name="gate_norm",
    )(y_f, y_b, proj, g_ssm.reshape(1, inner))


def _dft_tables(n):
    k = np.arange(n, dtype=np.int64)
    ang = 2.0 * np.pi * ((k[:, None] * k[None, :]) % n).astype(np.float64) / n
    return np.cos(ang), np.sin(ang)


def _chan_dft_kernel(a_ref, w_ref, o_ref):
    o_ref[...] = jnp.dot(a_ref[...], w_ref[...], preferred_element_type=F32).astype(o_ref.dtype)


def _chan_dft(f, gd):
    rows, width = f.shape
    groups = width // gd
    cc, sc = _dft_tables(gd)
    w = jnp.asarray(np.concatenate([cc, -sc], axis=1), BF16)
    tm = _pick_tile(rows, (1024, 512, 256, 128))
    return pl.pallas_call(
        _chan_dft_kernel,
        grid=(rows // tm, groups),
        in_specs=[pl.BlockSpec((tm, gd), lambda i, g: (i, g)),
                  pl.BlockSpec((gd, 2 * gd), lambda i, g: (0, 0))],
        out_specs=pl.BlockSpec((tm, 2 * gd), lambda i, g: (i, g)),
        out_shape=jax.ShapeDtypeStruct((rows, 2 * width), BF16),
        compiler_params=_params(2),
        name="fnet_chan",
    )(f, w)


def _pos_dft1_kernel(y_ref, w_ref, tc_ref, ts_ref, o_ref, *, groups, gd):
    n1 = y_ref.shape[0]
    p = jnp.dot(w_ref[...], y_ref[...], preferred_element_type=F32)
    tc = tc_ref[...]
    ts = ts_ref[...]
    for g in range(groups):
        re = slice(g * 2 * gd, g * 2 * gd + gd)
        im = slice(g * 2 * gd + gd, (g + 1) * 2 * gd)
        zr = p[0:n1, re] - p[n1:2 * n1, im]
        zi = p[0:n1, im] + p[n1:2 * n1, re]
        o_ref[:, re] = (zr * tc + zi * ts).astype(o_ref.dtype)
        o_ref[:, im] = (zi * tc - zr * ts).astype(o_ref.dtype)


def _pos_dft2_kernel(z_ref, w_ref, o_ref, *, groups, gd, scale):
    for g in range(groups):
        re = slice(g * 2 * gd, g * 2 * gd + gd)
        im = slice(g * 2 * gd + gd, (g + 1) * 2 * gd)
        zz = jnp.concatenate([z_ref[:, re], z_ref[:, im]], axis=0)
        out = jnp.dot(w_ref[...], zz, preferred_element_type=F32)
        o_ref[:, g * gd:(g + 1) * gd] = (out * scale).astype(o_ref.dtype)


def _pos_dft2(z, n2, groups, gd, scale):
    rows = z.shape[0]
    c2, s2 = _dft_tables(n2)
    w2 = jnp.asarray(np.concatenate([c2, s2], axis=1), BF16)
    return pl.pallas_call(
        functools.partial(_pos_dft2_kernel, groups=groups, gd=gd, scale=np.float32(scale)),
        grid=(rows // n2,),
        in_specs=[pl.BlockSpec((n2, groups * 2 * gd), lambda i: (i, 0)),
                  pl.BlockSpec((n2, 2 * n2), lambda i: (0, 0))],
        out_specs=pl.BlockSpec((n2, groups * gd), lambda i: (i, 0)),
        out_shape=jax.ShapeDtypeStruct((rows, groups * gd), BF16),
        compiler_params=_params(1),
        name="fnet_pos2",
    )(z, w2)


def _fourier_long(f, gd):
    length, width = f.shape
    groups = width // gd
    n2 = CHUNK
    n1 = length // n2
    cw = groups * 2 * gd
    y = _chan_dft(f, gd)
    yt = y.reshape(n1, n2, cw).transpose(1, 0, 2).reshape(length, cw)
    c1, s1 = _dft_tables(n1)
    w1 = jnp.asarray(np.concatenate([c1, -s1], axis=0), BF16)
    ang = 2.0 * np.pi * (np.arange(n2)[:, None] * np.arange(n1)[None, :]).astype(np.float64) / length
    tc = jnp.asarray(np.cos(ang)[:, :, None], F32)
    ts = jnp.asarray(np.sin(ang)[:, :, None], F32)
    zt = pl.pallas_call(
        functools.partial(_pos_dft1_kernel, groups=groups, gd=gd),
        grid=(n2,),
        in_specs=[pl.BlockSpec((n1, cw), lambda i: (i, 0)),
                  pl.BlockSpec((2 * n1, n1), lambda i: (0, 0)),
                  pl.BlockSpec((None, n1, 1), lambda i: (i, 0, 0)),
                  pl.BlockSpec((None, n1, 1), lambda i: (i, 0, 0))],
        out_specs=pl.BlockSpec((n1, cw), lambda i: (i, 0)),
        out_shape=jax.ShapeDtypeStruct((length, cw), BF16),
        compiler_params=_params(1),
        name="fnet_pos1",
    )(yt, w1, tc, ts)
    z = zt.reshape(n2, n1, cw).transpose(1, 0, 2).reshape(length, cw)
    o = _pos_dft2(z, n2, groups, gd, 1.0 / np.sqrt(float(length) * gd))
    return o.reshape(n1, n2, width).transpose(1, 0, 2).reshape(length, width)


def _fourier_short(f, gd):
    length, width = f.shape
    groups = width // gd
    y = _chan_dft(f, gd)
    return _pos_dft2(y, length, groups, gd, 1.0 / np.sqrt(float(length) * gd))


def _merge_kernel(ya_ref, yb_ref, yc_ref, g0_ref, g1_ref, g2_ref, wa_ref, wb_ref, wc_ref, o_ref):
    acc = g0_ref[...].astype(F32) * jnp.dot(ya_ref[...], wa_ref[...], preferred_element_type=F32)
    acc = acc + g1_ref[...].astype(F32) * jnp.dot(yb_ref[...], wb_ref[...], preferred_element_type=F32)
    acc = acc + g2_ref[...].astype(F32) * jnp.dot(yc_ref[...], wc_ref[...], preferred_element_type=F32)
    o_ref[...] = acc.astype(o_ref.dtype)


def _merge(ya, yb, yc, gates, w_pa, w_pb, w_pc):
    t_all = ya.shape[0]
    d_model = w_pa.shape[1]
    tn = 1024
    tm = _pick_tile(t_all, (640, 512, 256, 128))
    nblk = d_model // tn
    return pl.pallas_call(
        _merge_kernel,
        grid=(d_model // tn, t_all // tm),
        in_specs=[pl.BlockSpec((tm, ya.shape[1]), lambda j, i: (i, 0)),
                  pl.BlockSpec((tm, yb.shape[1]), lambda j, i: (i, 0)),
                  pl.BlockSpec((tm, yc.shape[1]), lambda j, i: (i, 0)),
                  pl.BlockSpec((tm, tn), lambda j, i: (i, j)),
                  pl.BlockSpec((tm, tn), lambda j, i: (i, nblk + j)),
                  pl.BlockSpec((tm, tn), lambda j, i: (i, 2 * nblk + j)),
                  pl.BlockSpec((w_pa.shape[0], tn), lambda j, i: (0, j)),
                  pl.BlockSpec((w_pb.shape[0], tn), lambda j, i: (0, j)),
                  pl.BlockSpec((w_pc.shape[0], tn), lambda j, i: (0, j))],
        out_specs=pl.BlockSpec((tm, tn), lambda j, i: (i, j)),
        out_shape=jax.ShapeDtypeStruct((t_all, d_model), BF16),
        compiler_params=_params(2),
        name="merge",
    )(ya, yb, yc, gates, gates, gates, w_pa.astype(BF16), w_pb.astype(BF16), w_pc.astype(BF16))


def _proj_res_kernel(a_ref, w_ref, h_ref, mod_ref, o_ref, *, k_gate, t_ctx, d_model):
    tm, tn = o_ref.shape
    j = pl.program_id(0)
    is_ctx = _row_is_ctx(pl.program_id(1), tm, t_ctx)
    acc = jnp.dot(a_ref[...], w_ref[...], preferred_element_type=F32)
    gate = jnp.where(is_ctx, mod_ref[1:2, :], mod_ref[0:1, :])
    o_ref[...] = h_ref[...] + gate * acc


def _proj_res(a, w, h, mod, layer, k_gate, t_ctx):
    t_all, d_model = h.shape
    tn = 512
    tm = _pick_tile(t_all, (640, 512, 256, 128))
    nblk = d_model // tn
    return pl.pallas_call(
        functools.partial(_proj_res_kernel, k_gate=k_gate, t_ctx=t_ctx, d_model=d_model),
        grid=(d_model // tn, t_all // tm),
        in_specs=[pl.BlockSpec((tm, a.shape[1]), lambda j, i: (i, 0)),
                  pl.BlockSpec((a.shape[1], tn), lambda j, i: (0, j)),
                  pl.BlockSpec((tm, tn), lambda j, i: (i, j)),
                  pl.BlockSpec((None, SUBLANES, tn), lambda j, i: (layer, 0, k_gate * nblk + j))],
        out_specs=pl.BlockSpec((tm, tn), lambda j, i: (i, j)),
        out_shape=jax.ShapeDtypeStruct((t_all, d_model), F32),
        compiler_params=_params(2),
        name="proj_res",
    )(a, w.astype(BF16), h, mod)


ROUTE_EID0, ROUTE_EID1, ROUTE_GATE0, ROUTE_GATE1 = 0, 1, 2, 3


def _ffn_norm_kernel(h_ref, g_ref, mod_ref, wr_ref, br_ref, x_ref, rt_ref, *, t_ctx):
    tm, d_model = h_ref.shape
    x = h_ref[...]
    y = x * lax.rsqrt(jnp.mean(x * x, axis=-1, keepdims=True) + EPS) * g_ref[...]
    is_ctx = _row_is_ctx(pl.program_id(0), tm, t_ctx)
    shift = _mod_rows(mod_ref, 3, d_model, is_ctx)
    scale = _mod_rows(mod_ref, 4, d_model, is_ctx)
    xn = y * (1.0 + scale) + shift
    n_tiles = d_model // LANES
    for j in range(n_tiles):
        x_ref[pl.ds(j, tm, stride=n_tiles), :] = xn[:, j * LANES:(j + 1) * LANES]
    logits = jnp.dot(xn, wr_ref[...], precision=HIGHEST, preferred_element_type=F32) + br_ref[...]
    lane = lax.broadcasted_iota(jnp.int32, logits.shape, 1).astype(F32)
    far = np.float32(ROUTER_COLS)
    neg = np.float32(-np.inf)
    lg = jnp.where(lane < MOE_GROUPS, logits, neg)
    m = jnp.max(lg, axis=-1, keepdims=True)
    g_top = jnp.min(jnp.where(lg == m, lane, far), axis=-1, keepdims=True)
    p_sel = 1.0 / jnp.sum(jnp.exp(lg - m), axis=-1, keepdims=True)
    lo = MOE_GROUPS + g_top * MOE_EPG
    ls = jnp.where(jnp.logical_and(lane >= lo, lane < lo + MOE_EPG), logits, neg)
    v1 = jnp.max(ls, axis=-1, keepdims=True)
    i1 = jnp.min(jnp.where(ls == v1, lane, far), axis=-1, keepdims=True)
    ls2 = jnp.where(lane == i1, neg, ls)
    v2 = jnp.max(ls2, axis=-1, keepdims=True)
    i2 = jnp.min(jnp.where(ls2 == v2, lane, far), axis=-1, keepdims=True)
    e2 = jnp.exp(v2 - v1)
    den = 1.0 / (1.0 + e2)
    rec = jnp.where(lane == ROUTE_EID0, i1 - MOE_GROUPS, 0.0)
    rec = jnp.where(lane == ROUTE_EID1, i2 - MOE_GROUPS, rec)
    rec = jnp.where(lane == ROUTE_GATE0, p_sel * den, rec)
    rec = jnp.where(lane == ROUTE_GATE1, p_sel * (e2 * den), rec)
    rt_ref[...] = rec


def _ffn_norm(h, g, mod, layer, w_router, b_router, t_ctx):
    t_all, d_model = h.shape
    tm = 256
    return pl.pallas_call(
        functools.partial(_ffn_norm_kernel, t_ctx=t_ctx),
        grid=(t_all // tm,),
        in_specs=[pl.BlockSpec((tm, d_model), lambda i: (i, 0)),
                  pl.BlockSpec((1, d_model), lambda i: (0, 0)),
                  pl.BlockSpec((None, SUBLANES, N_MOD * d_model), lambda i: (layer, 0, 0)),
                  pl.BlockSpec((d_model, ROUTER_COLS), lambda i: (0, 0)),
                  pl.BlockSpec((1, ROUTER_COLS), lambda i: (0, 0))],
        out_specs=[pl.BlockSpec((tm * (d_model // LANES), LANES), lambda i: (i, 0)),
                   pl.BlockSpec((tm, ROUTER_COLS), lambda i: (i, 0))],
        out_shape=[jax.ShapeDtypeStruct((t_all * (d_model // LANES), LANES), F32),
                   jax.ShapeDtypeStruct((t_all, ROUTER_COLS), F32)],
        compiler_params=_params(1),
        name="ffn_norm",
    )(h, g.reshape(1, d_model), mod, w_router, b_router)


def _moe_kernel(be_ref, nact_ref, first_ref, wslot_ref, nexte_ref,
                tok_ref, tok_next_ref, dst_ref, x_hbm, w1_hbm, w3_hbm, w2_hbm, bw_ref,
                y_hbm, xbuf0, xbuf1, ybuf0, ybuf1, wf1, wf3, wf2, w1b, w3b, w2b, sem_in, sem_out, sem_w,
                *, layer):
    b = pl.program_id(0)
    n_active = nact_ref[0]
    xbufs = (xbuf0, xbuf1)
    ybufs = (ybuf0, ybuf1)

    def weight_copies(e, ws):
        return [pltpu.make_async_copy(w1_hbm.at[layer, e], wf1.at[ws], sem_w.at[ws]),
                pltpu.make_async_copy(w3_hbm.at[layer, e], wf3.at[ws], sem_w.at[ws]),
                pltpu.make_async_copy(w2_hbm.at[layer, e], wf2.at[ws], sem_w.at[ws])]

    n_tiles = xbuf0.shape[0] // MOE_BLOCK

    def gather_copy(row0, r, s, sem_slot):
        return pltpu.make_async_copy(x_hbm.at[pl.ds(pl.multiple_of(row0, n_tiles), n_tiles), :],
                                     xbufs[s].at[pl.ds(r * n_tiles, n_tiles), :], sem_in.at[sem_slot])

    def scatter_copy(row0, r, s, sem_slot):
        return pltpu.make_async_copy(ybufs[s].at[pl.ds(r * n_tiles, n_tiles), :],
                                     y_hbm.at[pl.ds(pl.multiple_of(row0, n_tiles), n_tiles), :], sem_out.at[sem_slot])

    def wait_rows(copy):
        for _ in range(MOE_BLOCK):
            copy.wait()

    @pl.when(b == 0)
    def _():
        for cp in weight_copies(be_ref[0], 0):
            cp.start()
        for r in range(MOE_BLOCK):
            gather_copy(tok_ref[0, r], r, 0, 0).start()

    def step(s):
        @pl.when(first_ref[b] == 1)
        def _():
            ws = wslot_ref[b]
            for cp in weight_copies(be_ref[b], ws):
                cp.wait()
            w1b[...] = wf1[ws].astype(BF16)
            w3b[...] = wf3[ws].astype(BF16)
            w2b[...] = wf2[ws].astype(BF16)
            nxt = nexte_ref[b]

            @pl.when(nxt >= 0)
            def _():
                for cp in weight_copies(nxt, 1 - ws):
                    cp.start()

        wait_rows(gather_copy(0, 0, s, s))

        @pl.when(b >= 2)
        def _():
            wait_rows(scatter_copy(0, 0, s, s))

        for r in range(MOE_BLOCK):
            gather_copy(tok_next_ref[0, r], r, 1 - s, 1 - s).start()
        x = jnp.concatenate([xbufs[s][pl.ds(j, MOE_BLOCK, stride=n_tiles), :].astype(BF16)
                             for j in range(n_tiles)], axis=1)
        h1 = jnp.dot(x, w1b[...], preferred_element_type=F32)
        h3 = jnp.dot(x, w3b[...], preferred_element_type=F32)
        hh = (_silu(h1) * h3).astype(BF16)
        y = jnp.dot(hh, w2b[...], preferred_element_type=F32) * bw_ref[...]
        for j in range(n_tiles):
            ybufs[s][pl.ds(j, MOE_BLOCK, stride=n_tiles), :] = y[:, j * LANES:(j + 1) * LANES]
        for r in range(MOE_BLOCK):
            scatter_copy(dst_ref[0, r], r, s, s).start()

    active = b < n_active
    for s in range(2):
        pl.when(jnp.logical_and(active, b % 2 == s))(functools.partial(step, s))

    @pl.when(b == n_active - 1)
    def _():
        sb = b % 2
        wait_rows(gather_copy(0, 0, 0, 1 - sb))

        @pl.when(b >= 1)
        def _():
            wait_rows(scatter_copy(0, 0, 0, 1 - sb))
        wait_rows(scatter_copy(0, 0, 0, sb))
        ybuf0[...] = jnp.zeros(ybuf0.shape, F32)
        rows = MOE_BLOCK * n_tiles
        spare0 = y_hbm.shape[0] - 2 * rows
        fills = [pltpu.make_async_copy(ybuf0, y_hbm.at[pl.ds(spare0 + k * rows, rows), :], sem_out.at[0])
                 for k in range(2)]
        for cp in fills:
            cp.start()
        for cp in fills:
            cp.wait()


def _moe_blocks(layer, tables, xn, w1, w3, w2, n_out_rows):
    block_e, n_active, first, wslot, next_e, tok_idx, dst_idx, buf_w = tables
    d_model, ff = w1.shape[2], w1.shape[3]
    n_tiles = d_model // LANES
    n_blocks = tok_idx.shape[0]
    slab = (MOE_BLOCK * n_tiles, LANES)
    idx_spec = functools.partial(pl.BlockSpec, (None, 1, MOE_BLOCK), memory_space=pltpu.SMEM)
    grid_spec = pltpu.PrefetchScalarGridSpec(
        num_scalar_prefetch=5,
        grid=(n_blocks,),
        in_specs=[idx_spec(index_map=lambda b, *_: (b, 0, 0)),
                  idx_spec(index_map=lambda b, *_: (jnp.minimum(b + 1, n_blocks - 1), 0, 0)),
                  idx_spec(index_map=lambda b, *_: (b, 0, 0)),
                  pl.BlockSpec(memory_space=pl.ANY),
                  pl.BlockSpec(memory_space=pl.ANY),
                  pl.BlockSpec(memory_space=pl.ANY),
                  pl.BlockSpec(memory_space=pl.ANY),
                  pl.BlockSpec((MOE_BLOCK, 1), lambda b, *_: (b, 0))],
        out_specs=pl.BlockSpec(memory_space=pl.ANY),
        scratch_shapes=[pltpu.VMEM(slab, F32), pltpu.VMEM(slab, F32), pltpu.VMEM(slab, F32), pltpu.VMEM(slab, F32),
                        pltpu.VMEM((2, d_model, ff), F32), pltpu.VMEM((2, d_model, ff), F32),
                        pltpu.VMEM((2, ff, d_model), F32),
                        pltpu.VMEM((d_model, ff), BF16), pltpu.VMEM((d_model, ff), BF16),
                        pltpu.VMEM((ff, d_model), BF16),
                        pltpu.SemaphoreType.DMA((2,)), pltpu.SemaphoreType.DMA((2,)),
                        pltpu.SemaphoreType.DMA((2,))],
    )
    return pl.pallas_call(
        functools.partial(_moe_kernel, layer=layer),
        grid_spec=grid_spec,
        out_shape=jax.ShapeDtypeStruct((n_out_rows * n_tiles, LANES), F32),
        compiler_params=_params(1),
        name="moe_blocks",
    )(block_e, n_active, first, wslot, next_e, tok_idx, tok_idx, dst_idx, xn, w1, w3, w2, buf_w.reshape(-1, 1))


def _dispatch(route, t_all, n_tiles):
    eid = route[:, ROUTE_EID0:ROUTE_EID1 + 1].astype(jnp.int32)
    gate = route[:, ROUTE_GATE0:ROUTE_GATE1 + 1]
    n_asg = t_all * MOE_TOPK
    flat_e = eid.reshape(n_asg)
    flat_w = gate.reshape(n_asg)
    order = jnp.argsort(flat_e).astype(jnp.int32)
    experts = jnp.arange(MOE_EXPERTS, dtype=jnp.int32)
    counts = jnp.sum((flat_e[:, None] == experts[None, :]).astype(jnp.int32), axis=0)
    padded = (counts + MOE_BLOCK - 1) // MOE_BLOCK * MOE_BLOCK
    pad_end = jnp.cumsum(padded)
    pad_start = pad_end - padded
    start = jnp.cumsum(counts) - counts
    n_blocks = (n_asg + MOE_EXPERTS * (MOE_BLOCK - 1)) // MOE_BLOCK + 1
    blk = jnp.arange(n_blocks, dtype=jnp.int32)
    blk_e = jnp.minimum(jnp.sum((pad_end[None, :] <= (blk * MOE_BLOCK)[:, None]).astype(jnp.int32), axis=1),
                        MOE_EXPERTS - 1)
    n_active = pad_end[-1] // MOE_BLOCK
    is_act = blk < n_active
    block_e = jnp.where(is_act, blk_e, 0)
    prev_e = jnp.concatenate([jnp.full((1,), -1, jnp.int32), blk_e[:-1]])
    first = jnp.logical_and(is_act, blk_e != prev_e).astype(jnp.int32)
    has_tok = counts > 0
    rank = jnp.cumsum(has_tok.astype(jnp.int32)) - 1
    wslot = rank[blk_e] % 2
    later = lax.cummin(jnp.where(has_tok, experts, MOE_EXPERTS)[::-1])[::-1]
    nxt = jnp.concatenate([later[1:], jnp.full((1,), MOE_EXPERTS, jnp.int32)])
    next_e = jnp.where(nxt >= MOE_EXPERTS, -1, nxt)[blk_e]
    r_in_blk = jnp.arange(MOE_BLOCK, dtype=jnp.int32)[None, :]
    r = (blk * MOE_BLOCK - pad_start[blk_e])[:, None] + r_in_blk
    valid = jnp.logical_and(r < counts[blk_e][:, None], (blk < n_active)[:, None])
    p = jnp.clip(start[blk_e][:, None] + r, 0, n_asg - 1)
    a = jnp.take(order, p, mode="clip")
    tok = jnp.where(valid, a >> 1, 0)
    spare = MOE_TOPK * t_all + (blk % 2)[:, None] * MOE_BLOCK + r_in_blk
    dst = jnp.where(valid, (a & 1) * t_all + (a >> 1), spare)
    w = jnp.where(valid, jnp.take(flat_w, a, mode="clip"), 0.0)
    return (block_e.astype(jnp.int32), n_active.astype(jnp.int32).reshape(1),
            first, wslot.astype(jnp.int32), next_e.astype(jnp.int32),
            (tok * n_tiles).reshape(n_blocks, 1, MOE_BLOCK).astype(jnp.int32),
            (dst * n_tiles).reshape(n_blocks, 1, MOE_BLOCK).astype(jnp.int32), w.reshape(-1))


def _moe_res_norm_kernel(h_ref, y0_ref, y1_ref, mod_ref, g_ref, *rest, t_ctx, tile0, last):
    tm, d_model = h_ref.shape
    n_tiles = d_model // LANES
    is_ctx = _row_is_ctx(pl.program_id(0) + tile0, tm, t_ctx)
    ysum = jnp.concatenate([y0_ref[pl.ds(j, tm, stride=n_tiles), :] + y1_ref[pl.ds(j, tm, stride=n_tiles), :]
                            for j in range(n_tiles)], axis=1)
    h = h_ref[...] + _mod_rows(mod_ref, 5, d_model, is_ctx) * ysum
    y = h * lax.rsqrt(jnp.mean(h * h, axis=-1, keepdims=True) + EPS) * g_ref[...]
    if last:
        (o_ref,) = rest
        o_ref[...] = y
    else:
        nmod_ref, ho_ref, hn_ref = rest
        ho_ref[...] = h
        shift = _mod_rows(nmod_ref, 0, d_model, is_ctx)
        scale = _mod_rows(nmod_ref, 1, d_model, is_ctx)
        hn_ref[...] = (y * (1.0 + scale) + shift).astype(hn_ref.dtype)


def _moe_res_norm(h, y01, mod, layer, g_next, t_ctx, last):
    t_all, d_model = h.shape
    tm = 256
    nt = t_all // tm
    tile0 = t_ctx // tm if last else 0
    row = pl.BlockSpec((tm, d_model), lambda i: (i + tile0, 0))
    yrows = (tm * (d_model // LANES), LANES)
    mod_spec = lambda lyr: pl.BlockSpec((None, SUBLANES, N_MOD * d_model), lambda i: (lyr, 0, 0))
    in_specs = [row, pl.BlockSpec(yrows, lambda i: (i + tile0, 0)),
                pl.BlockSpec(yrows, lambda i: (i + tile0 + nt, 0)), mod_spec(layer),
                pl.BlockSpec((1, d_model), lambda i: (0, 0))]
    args = [h, y01, y01, mod, g_next.reshape(1, d_model)]
    out_row = pl.BlockSpec((tm, d_model), lambda i: (i, 0))
    if last:
        out_specs = out_row
        out_shape = jax.ShapeDtypeStruct((t_all - t_ctx, d_model), F32)
    else:
        in_specs.append(mod_spec(layer + 1))
        args.append(mod)
        out_specs = [out_row, out_row]
        out_shape = [jax.ShapeDtypeStruct((t_all, d_model), F32), jax.ShapeDtypeStruct((t_all, d_model), BF16)]
    return pl.pallas_call(
        functools.partial(_moe_res_norm_kernel, t_ctx=t_ctx, tile0=tile0, last=last),
        grid=(nt - tile0,),
        in_specs=in_specs,
        out_specs=out_specs,
        out_shape=out_shape,
        compiler_params=_params(1),
        name="moe_res_norm",
    )(*args)


def kernel(x, c, ctx, c_ctx, w_ada, b_ada, g_mix, w_in, b_gate, ln_a_g, ln_a_b, w_sp, b_sp, conv_w, conv_b, dt_bias, a_log, d_skip, g_ssm, w_pa, w_pb, w_pc, w_out, g_ffn, w_rg, b_rg, w_re, b_re, w_e1, w_e3, w_e2, g_final):
    bsz, t_lat, d_model = x.shape
    assert bsz == 1 and c.shape[0] == 1
    t_ctx = ctx.shape[1]
    depth = w_ada.shape[0]
    a_width = ln_a_g.shape[1]
    inner = SSM_HEADS * SSM_HEAD_DIM
    conv_dim = conv_w.shape[2]
    f_width = w_pc.shape[1]
    f_gd = f_width // F_GROUPS
    n_ctx_chunks = t_ctx // CHUNK
    t_all = t_ctx + t_lat
    assert t_ctx % 256 == 0 and t_lat % 256 == 0 and t_lat % (CHUNK * SUBLANES) == 0

    c_u, c_z = 0, 2 * a_width
    c_xbc = c_z + inner
    c_dt = c_xbc + conv_dim
    c_fc = c_dt + 2 * SSM_HEADS
    c_gl = c_fc + f_width
    n_gl = N_BRANCH * d_model
    p_z, p_xbc = 2 * a_width, 2 * a_width + inner

    cs = jnp.zeros((SUBLANES, d_model), F32).at[0].set(c[0]).at[1].set(c_ctx)
    mod = _ada(cs, w_ada, b_ada)

    h = jnp.concatenate([ctx[0], x[0]], axis=0)
    pad_heads = jnp.zeros((128 - 2 * SSM_HEADS,), F32)

    for l in range(depth):
        wl = w_in[l]
        w_main = wl[:, c_u:c_dt].astype(BF16)
        w_gl = wl[:, c_gl:c_gl + n_gl].astype(BF16)
        w_dt = jnp.concatenate([wl[:, c_dt:c_fc], jnp.zeros((d_model, 128 - 2 * SSM_HEADS), F32)], axis=1).astype(BF16)
        w_fc = wl[:, c_fc:c_gl].astype(BF16)

        if l == 0:
            hn = _norm_mod(h, g_mix[l], mod, l, 0, 1, t_ctx, BF16)
        proj = _mm(hn, w_main, F32, 1024, "proj_main")
        gates = _mm_gate(hn, w_gl, b_gate[l], 1024)
        dt_raw = _mm(hn, w_dt, F32, 128, "proj_dt")
        fc = _mm(hn, w_fc, BF16, 1024, "proj_fc")

        ya = _sgu(proj, ln_a_g[l], ln_a_b[l], w_sp[l], b_sp[l], a_width)

        dtb = jnp.concatenate([dt_bias[l].reshape(-1), pad_heads]).reshape(1, 128)
        alg = jnp.concatenate([a_log[l].reshape(-1), pad_heads]).reshape(1, 128)
        dsk = jnp.repeat(d_skip[l], SSM_HEAD_DIM, axis=-1)
        assert p_xbc % conv_dim == 0 and p_z % inner == 0
        y_f = _ssd(proj, p_xbc // conv_dim, dt_raw, conv_w[l], conv_b[l], dtb, alg, dsk[0:1], 0, n_ctx_chunks)
        y_b = _ssd(proj, p_xbc // conv_dim, dt_raw, conv_w[l], conv_b[l], dtb, alg, dsk[1:2], 1, n_ctx_chunks)
        yb = _gate_norm(y_f, y_b, proj, p_z // inner, g_ssm[l])

        yc_lat = _fourier_long(fc[t_ctx:], f_gd)
        if l < depth - 1:
            yc_ctx = _fourier_short(fc[:t_ctx], f_gd)
        else:
            yc_ctx = jnp.zeros((t_ctx, f_width), BF16)
        yc = jnp.concatenate([yc_ctx, yc_lat], axis=0)

        merged = _merge(ya, yb, yc, gates, w_pa[l], w_pb[l], w_pc[l])
        h = _proj_res(merged, w_out[l], h, mod, l, 2, t_ctx)

        w_router = jnp.concatenate([w_rg[l], w_re[l], jnp.zeros((d_model, ROUTER_COLS - MOE_GROUPS - MOE_EXPERTS), F32)], axis=1)
        b_router = jnp.concatenate([b_rg[l], b_re[l], jnp.zeros((ROUTER_COLS - MOE_GROUPS - MOE_EXPERTS,), F32)]).reshape(1, ROUTER_COLS)
        xn, route = _ffn_norm(h, g_ffn[l], mod, l, w_router, b_router, t_ctx)
        y01 = _moe_blocks(l, _dispatch(route, t_all, d_model // LANES), xn, w_e1, w_e3, w_e2,
                          MOE_TOPK * t_all + 2 * MOE_BLOCK)
        if l < depth - 1:
            h, hn = _moe_res_norm(h, y01, mod, l, g_mix[l + 1], t_ctx, False)
        else:
            out = _moe_res_norm(h, y01, mod, l, g_final, t_ctx, True)
    return out.reshape(bsz, t_lat, d_model)
```

```python
import functools

import numpy as np
import jax
import jax.numpy as jnp
from jax import lax
from jax.experimental import pallas as pl
from jax.experimental.pallas import tpu as pltpu

F32 = jnp.float32
BF16 = jnp.bfloat16
HIGHEST = lax.Precision.HIGHEST

EPS = 1e-6
N_MOD = 6
GRID_W = 64
CHUNK = 128
A_GROUPS = 8
SSM_HEADS = 32
SSM_HEAD_DIM = 64
SSM_GROUPS = 8
SSM_STATE = 128
F_GROUPS = 4
N_BRANCH = 3
MOE_GROUPS = 8
MOE_EPG = 8
MOE_EXPERTS = 64
MOE_TOPK = 2
MOE_BLOCK = 128
ROUTER_COLS = 128

V7X_VMEM_LIMIT_BYTES = 56 * 1024 * 1024
SUBLANES = 8
LANES = 128


def _params(n_axes):
    return pltpu.CompilerParams(dimension_semantics=("arbitrary",) * n_axes,
                                vmem_limit_bytes=V7X_VMEM_LIMIT_BYTES)


def _pick_tile(n, candidates):
    for t in candidates:
        if n % t == 0:
            return t
    raise ValueError(f"no tile for {n}")


def _gelu(x):
    return 0.5 * x * (1.0 + lax.erf(x * np.float32(1.0 / np.sqrt(2.0))))


def _silu(x):
    return x * jax.nn.sigmoid(x)


def _softplus(x):
    return jnp.maximum(x, 0.0) + jnp.log(1.0 + jnp.exp(-jnp.abs(x)))


def _row_is_ctx(tile_index, tm, t_ctx):
    rows = tile_index * tm + lax.broadcasted_iota(jnp.int32, (tm, 1), 0)
    return rows < t_ctx


def _mod_rows(mod_ref, k, d_model, is_ctx):
    lo = k * d_model
    lat = mod_ref[0:1, lo:lo + d_model]
    ctx = mod_ref[1:2, lo:lo + d_model]
    return jnp.where(is_ctx, ctx, lat)


def _ada_kernel(cs_ref, w_ref, b_ref, o_ref):
    s = _silu(cs_ref[...])
    o_ref[...] = jnp.dot(s, w_ref[...], precision=HIGHEST, preferred_element_type=F32) + b_ref[...]


def _ada(cs, w_ada, b_ada):
    depth, d_model, n = w_ada.shape
    tn = 1024
    return pl.pallas_call(
        _ada_kernel,
        grid=(depth, n // tn),
        in_specs=[pl.BlockSpec((SUBLANES, d_model), lambda l, j: (0, 0)),
                  pl.BlockSpec((None, d_model, tn), lambda l, j: (l, 0, j)),
                  pl.BlockSpec((None, 1, tn), lambda l, j: (l, 0, j))],
        out_specs=pl.BlockSpec((None, SUBLANES, tn), lambda l, j: (l, 0, j)),
        out_shape=jax.ShapeDtypeStruct((depth, SUBLANES, n), F32),
        compiler_params=_params(2),
        name="ada",
    )(cs, w_ada, b_ada.reshape(depth, 1, n))


def _norm_mod_kernel(h_ref, g_ref, mod_ref, o_ref, *, k_shift, k_scale, t_ctx):
    tm, d_model = h_ref.shape
    x = h_ref[...]
    y = x * lax.rsqrt(jnp.mean(x * x, axis=-1, keepdims=True) + EPS) * g_ref[...]
    is_ctx = _row_is_ctx(pl.program_id(0), tm, t_ctx)
    shift = _mod_rows(mod_ref, k_shift, d_model, is_ctx)
    scale = _mod_rows(mod_ref, k_scale, d_model, is_ctx)
    o_ref[...] = (y * (1.0 + scale) + shift).astype(o_ref.dtype)


def _norm_mod(h, g, mod, layer, k_shift, k_scale, t_ctx, out_dtype):
    t_all, d_model = h.shape
    tm = 256
    return pl.pallas_call(
        functools.partial(_norm_mod_kernel, k_shift=k_shift, k_scale=k_scale, t_ctx=t_ctx),
        grid=(t_all // tm,),
        in_specs=[pl.BlockSpec((tm, d_model), lambda i: (i, 0)),
                  pl.BlockSpec((1, d_model), lambda i: (0, 0)),
                  pl.BlockSpec((None, SUBLANES, N_MOD * d_model), lambda i: (layer, 0, 0))],
        out_specs=pl.BlockSpec((tm, d_model), lambda i: (i, 0)),
        out_shape=jax.ShapeDtypeStruct((t_all, d_model), out_dtype),
        compiler_params=_params(1),
        name="norm_mod",
    )(h, g.reshape(1, d_model), mod)


def _mm_kernel(a_ref, w_ref, o_ref):
    o_ref[...] = jnp.dot(a_ref[...], w_ref[...], preferred_element_type=F32).astype(o_ref.dtype)


def _mm(a, w, out_dtype, tn, name):
    m, k = a.shape
    n = w.shape[1]
    tm = _pick_tile(m, (1280, 1024, 640, 512, 256, 128))
    return pl.pallas_call(
        _mm_kernel,
        grid=(n // tn, m // tm),
        in_specs=[pl.BlockSpec((tm, k), lambda j, i: (i, 0)),
                  pl.BlockSpec((k, tn), lambda j, i: (0, j))],
        out_specs=pl.BlockSpec((tm, tn), lambda j, i: (i, j)),
        out_shape=jax.ShapeDtypeStruct((m, n), out_dtype),
        compiler_params=_params(2),
        name=name,
    )(a, w)


def _mm_gate_kernel(a_ref, w_ref, b_ref, o_ref):
    acc = jnp.dot(a_ref[...], w_ref[...], preferred_element_type=F32)
    o_ref[...] = jax.nn.sigmoid(acc + b_ref[...]).astype(o_ref.dtype)


def _mm_gate(a, w, b, tn):
    m, k = a.shape
    n = w.shape[1]
    tm = _pick_tile(m, (1280, 1024, 640, 512, 256, 128))
    return pl.pallas_call(
        _mm_gate_kernel,
        grid=(n // tn, m // tm),
        in_specs=[pl.BlockSpec((tm, k), lambda j, i: (i, 0)),
                  pl.BlockSpec((k, tn), lambda j, i: (0, j)),
                  pl.BlockSpec((1, tn), lambda j, i: (0, j))],
        out_specs=pl.BlockSpec((tm, tn), lambda j, i: (i, j)),
        out_shape=jax.ShapeDtypeStruct((m, n), BF16),
        compiler_params=_params(2),
        name="proj_gate",
    )(a, w, b.reshape(1, n))


def _sgu_kernel(u_ref, v_ref, g_ref, b_ref, wsp_ref, bsp_ref, o_ref):
    v = _gelu(v_ref[...])
    mu = jnp.mean(v, axis=-1, keepdims=True)
    var = jnp.mean(jnp.square(v - mu), axis=-1, keepdims=True)
    vn = ((v - mu) * lax.rsqrt(var + EPS)) * g_ref[...] + b_ref[...]
    vb = vn.astype(BF16)
    gd = v.shape[-1] // A_GROUPS
    for g in range(A_GROUPS):
        cols = slice(g * gd, (g + 1) * gd)
        s = jnp.dot(wsp_ref[g], vb[:, cols], preferred_element_type=F32) + bsp_ref[:, g:g + 1]
        o_ref[:, cols] = (_gelu(u_ref[:, cols]) * s).astype(o_ref.dtype)


def _sgu(proj, ln_g, ln_b, w_sp, b_sp, a_width):
    t_all = proj.shape[0]
    return pl.pallas_call(
        _sgu_kernel,
        grid=(t_all // CHUNK,),
        in_specs=[pl.BlockSpec((CHUNK, a_width), lambda c: (c, 0)),
                  pl.BlockSpec((CHUNK, a_width), lambda c: (c, 1)),
                  pl.BlockSpec((1, a_width), lambda c: (0, 0)),
                  pl.BlockSpec((1, a_width), lambda c: (0, 0)),
                  pl.BlockSpec((A_GROUPS, CHUNK, CHUNK), lambda c: (0, 0, 0)),
                  pl.BlockSpec((CHUNK, A_GROUPS), lambda c: (0, 0))],
        out_specs=pl.BlockSpec((CHUNK, a_width), lambda c: (c, 0)),
        out_shape=jax.ShapeDtypeStruct((t_all, a_width), BF16),
        compiler_params=_params(1),
        name="sgu",
    )(proj, proj, ln_g.reshape(1, a_width), ln_b.reshape(1, a_width), w_sp.astype(BF16), b_sp.T)


def _ssd_chunk_of(i, direction, n_ctx_chunks, n_chunks):
    if direction == 0:
        return i
    return jnp.where(i < n_ctx_chunks, n_ctx_chunks - 1 - i, n_chunks - 1 - (i - n_ctx_chunks))


def _ssd_kernel(xbc_ref, prev_ref, next_ref, dt_ref, cw_ref, cb_ref, dtb_ref, alog_ref, dsk_ref,
                o_ref, state_ref, *, direction, n_ctx_chunks, n_chunks):
    L = CHUNK
    i = pl.program_id(0)
    c = _ssd_chunk_of(i, direction, n_ctx_chunks, n_chunks)
    is_first = jnp.logical_or(c == 0, c == n_ctx_chunks)
    is_last = jnp.logical_or(c == n_ctx_chunks - 1, c == n_chunks - 1)

    @pl.when(i == 0)
    def _():
        state_ref[...] = jnp.zeros_like(state_ref)

    row = lax.broadcasted_iota(jnp.int32, (L, L), 0)
    col = lax.broadcasted_iota(jnp.int32, (L, L), 1)
    causal = (col <= row) if direction == 0 else (col >= row)
    row1 = lax.broadcasted_iota(jnp.int32, (L, 1), 0)

    dt = _softplus(dt_ref[...] + dtb_ref[...])
    a = -jnp.exp(alog_ref[...])
    tri = jnp.where(causal, 1.0, 0.0).astype(BF16)
    acs = jnp.zeros((L, dt.shape[1]), F32)
    rem = dt * a
    for _ in range(3):
        part = rem.astype(BF16)
        acs = acs + jnp.dot(tri, part, preferred_element_type=F32)
        rem = rem - part.astype(F32)
    acs_end = acs[L - 1:L, :] if direction == 0 else acs[0:1, :]
    m_all = dt * jnp.exp(acs_end - acs)
    e_in = jnp.exp(acs)
    chunk_decay = jnp.exp(acs_end)
    acs_t = acs.T
    dt_t = dt.T

    def conv_silu(c0, w):
        x = xbc_ref[:, c0:c0 + w]
        pr = jnp.where(is_first, 0.0, prev_ref[SUBLANES - 1:SUBLANES, c0:c0 + w])
        nx = jnp.where(is_last, 0.0, next_ref[0:1, c0:c0 + w])
        xm = jnp.where(row1 == 0, pr, pltpu.roll(x, 1, 0))
        xp = jnp.where(row1 == L - 1, nx, pltpu.roll(x, L - 1, 0))
        y = (xm * cw_ref[0:1, c0:c0 + w] + x * cw_ref[1:2, c0:c0 + w]
             + xp * cw_ref[2:3, c0:c0 + w] + cb_ref[:, c0:c0 + w])
        return _silu(y)

    inner = SSM_HEADS * SSM_HEAD_DIM
    bc = SSM_GROUPS * SSM_STATE
    pair_w = 2 * SSM_HEAD_DIM
    lane = lax.broadcasted_iota(jnp.int32, (L, pair_w), 1)
    lo_half = lane < SSM_HEAD_DIM
    prow = lax.broadcasted_iota(jnp.int32, (pair_w, 1), 0)
    lo_rows = prow < SSM_HEAD_DIM

    for g in range(SSM_GROUPS):
        b_g = conv_silu(inner + g * SSM_STATE, SSM_STATE)
        c_g = conv_silu(inner + bc + g * SSM_STATE, SSM_STATE)
        c_gb = c_g.astype(BF16)
        b_gb = b_g.astype(BF16)
        cb = lax.dot_general(c_gb, b_gb, (((1,), (1,)), ((), ())), preferred_element_type=F32)
        for jj in range(2):
            j = 2 * g + jj
            c0 = direction * SSM_HEADS + 2 * j
            x2f = conv_silu(j * pair_w, pair_w)
            x2 = x2f.astype(BF16)
            zero = jnp.zeros_like(x2)
            xd = jnp.concatenate([jnp.where(lo_half, x2, zero), jnp.where(lo_half, zero, x2)], axis=0)
            l_parts = []
            for cc in (c0, c0 + 1):
                seg = acs[:, cc:cc + 1] - acs_t[cc:cc + 1, :]
                dec = jnp.exp(jnp.where(causal, seg, -jnp.inf))
                l_parts.append((cb * dec * dt_t[cc:cc + 1, :]).astype(BF16))
            l2 = jnp.concatenate(l_parts, axis=1)
            e2 = jnp.where(lo_half, e_in[:, c0:c0 + 1], e_in[:, c0 + 1:c0 + 2])
            m2 = jnp.where(lo_half, m_all[:, c0:c0 + 1], m_all[:, c0 + 1:c0 + 2])
            sp = state_ref[j]
            y = jnp.dot(l2, xd, preferred_element_type=F32)
            y = y + e2 * lax.dot_general(c_gb, sp.astype(BF16), (((1,), (1,)), ((), ())),
                                         preferred_element_type=F32)
            y = y + dsk_ref[:, j * pair_w:(j + 1) * pair_w] * x2f
            upd = lax.dot_general((x2f * m2).astype(BF16), b_gb, (((0,), (0,)), ((), ())),
                                  preferred_element_type=F32)
            cd = jnp.where(lo_rows, chunk_decay[:, c0:c0 + 1], chunk_decay[:, c0 + 1:c0 + 2])
            state_ref[j] = cd * sp + upd
            o_ref[:, j * pair_w:(j + 1) * pair_w] = y


def _ssd(proj, xbc_blk0, dt_raw, conv_w, conv_b, dt_bias_p, a_log_p, d_skip_x, direction, n_ctx_chunks):
    t_all = proj.shape[0]
    n_chunks = t_all // CHUNK
    inner = SSM_HEADS * SSM_HEAD_DIM
    conv_dim = conv_w.shape[1]
    chunk_of = functools.partial(_ssd_chunk_of, direction=direction, n_ctx_chunks=n_ctx_chunks,
                                 n_chunks=n_chunks)
    rows8 = CHUNK // SUBLANES
    last8 = t_all // SUBLANES - 1
    in_specs = [
        pl.BlockSpec((CHUNK, conv_dim), lambda i: (chunk_of(i), xbc_blk0)),
        pl.BlockSpec((SUBLANES, conv_dim), lambda i: (jnp.maximum(chunk_of(i) * rows8 - 1, 0), xbc_blk0)),
        pl.BlockSpec((SUBLANES, conv_dim), lambda i: (jnp.minimum(chunk_of(i) * rows8 + rows8, last8), xbc_blk0)),
        pl.BlockSpec((CHUNK, 128), lambda i: (chunk_of(i), 0)),
        pl.BlockSpec((3, conv_dim), lambda i: (0, 0)),
        pl.BlockSpec((1, conv_dim), lambda i: (0, 0)),
        pl.BlockSpec((1, 128), lambda i: (0, 0)),
        pl.BlockSpec((1, 128), lambda i: (0, 0)),
        pl.BlockSpec((1, inner), lambda i: (0, 0)),
    ]
    return pl.pallas_call(
        functools.partial(_ssd_kernel, direction=direction, n_ctx_chunks=n_ctx_chunks, n_chunks=n_chunks),
        grid=(n_chunks,),
        in_specs=in_specs,
        out_specs=pl.BlockSpec((CHUNK, inner), lambda i: (chunk_of(i), 0)),
        out_shape=jax.ShapeDtypeStruct((t_all, inner), F32),
        scratch_shapes=[pltpu.VMEM((SSM_HEADS // 2, 2 * SSM_HEAD_DIM, SSM_STATE), F32)],
        compiler_params=_params(1),
        name="ssd_bwd" if direction else "ssd_fwd",
    )(proj, proj, proj, dt_raw, conv_w, conv_b.reshape(1, conv_dim), dt_bias_p, a_log_p, d_skip_x)


def _gate_norm_kernel(yf_ref, yb_ref, z_ref, g_ref, o_ref):
    t = (yf_ref[...] + yb_ref[...]) * _silu(z_ref[...])
    scale = lax.rsqrt(jnp.mean(t * t, axis=-1, keepdims=True) + EPS)
    o_ref[...] = (t * scale * g_ref[...]).astype(o_ref.dtype)


def _gate_norm(y_f, y_b, proj, z_blk0, g_ssm):
    t_all, inner = y_f.shape
    tm = 256
    row = pl.BlockSpec((tm, inner), lambda i: (i, 0))
    return pl.pallas_call(
        _gate_norm_kernel,
        grid=(t_all // tm,),
        in_specs=[row, row, pl.BlockSpec((tm, inner), lambda i: (i, z_blk0)),
                  pl.BlockSpec((1, inner), lambda i: (0, 0))],
        out_specs=row,
        out_shape=jax.ShapeDtypeStruct((t_all, inner), BF16),
        compiler_params=_params(1),
        name="gate_norm",
    )(y_f, y_b, proj, g_ssm.reshape(1, inner))


def _dft_tables(n):
    k = np.arange(n, dtype=np.int64)
    ang = 2.0 * np.pi * ((k[:, None] * k[None, :]) % n).astype(np.float64) / n
    return np.cos(ang), np.sin(ang)


def _chan_dft_kernel(a_ref, w_ref, o_ref):
    o_ref[...] = jnp.dot(a_ref[...], w_ref[...], preferred_element_type=F32).astype(o_ref.dtype)


def _chan_dft(f, gd):
    rows, width = f.shape
    groups = width // gd
    cc, sc = _dft_tables(gd)
    w = jnp.asarray(np.concatenate([cc, -sc], axis=1), BF16)
    tm = _pick_tile(rows, (1024, 512, 256, 128))
    return pl.pallas_call(
        _chan_dft_kernel,
        grid=(rows // tm, groups),
        in_specs=[pl.BlockSpec((tm, gd), lambda i, g: (i, g)),
                  pl.BlockSpec((gd, 2 * gd), lambda i, g: (0, 0))],
        out_specs=pl.BlockSpec((tm, 2 * gd), lambda i, g: (i, g)),
        out_shape=jax.ShapeDtypeStruct((rows, 2 * width), BF16),
        compiler_params=_params(2),
        name="fnet_chan",
    )(f, w)


def _pos_dft1_kernel(y_ref, w_ref, tc_ref, ts_ref, o_ref, *, groups, gd):
    n1 = y_ref.shape[0]
    p = jnp.dot(w_ref[...], y_ref[...], preferred_element_type=F32)
    tc = tc_ref[...]
    ts = ts_ref[...]
    for g in range(groups):
        re = slice(g * 2 * gd, g * 2 * gd + gd)
        im = slice(g * 2 * gd + gd, (g + 1) * 2 * gd)
        zr = p[0:n1, re] - p[n1:2 * n1, im]
        zi = p[0:n1, im] + p[n1:2 * n1, re]
        o_ref[:, re] = (zr * tc + zi * ts).astype(o_ref.dtype)
        o_ref[:, im] = (zi * tc - zr * ts).astype(o_ref.dtype)


def _pos_dft2_kernel(z_ref, w_ref, o_ref, *, groups, gd, scale):
    for g in range(groups):
        re = slice(g * 2 * gd, g * 2 * gd + gd)
        im = slice(g * 2 * gd + gd, (g + 1) * 2 * gd)
        zz = jnp.concatenate([z_ref[:, re], z_ref[:, im]], axis=0)
        out = jnp.dot(w_ref[...], zz, preferred_element_type=F32)
        o_ref[:, g * gd:(g + 1) * gd] = (out * scale).astype(o_ref.dtype)


def _pos_dft2(z, n2, groups, gd, scale):
    rows = z.shape[0]
    c2, s2 = _dft_tables(n2)
    w2 = jnp.asarray(np.concatenate([c2, s2], axis=1), BF16)
    return pl.pallas_call(
        functools.partial(_pos_dft2_kernel, groups=groups, gd=gd, scale=np.float32(scale)),
        grid=(rows // n2,),
        in_specs=[pl.BlockSpec((n2, groups * 2 * gd), lambda i: (i, 0)),
                  pl.BlockSpec((n2, 2 * n2), lambda i: (0, 0))],
        out_specs=pl.BlockSpec((n2, groups * gd), lambda i: (i, 0)),
        out_shape=jax.ShapeDtypeStruct((rows, groups * gd), BF16),
        compiler_params=_params(1),
        name="fnet_pos2",
    )(z, w2)


def _fourier_long(f, gd):
    length, width = f.shape
    groups = width // gd
    n2 = CHUNK
    n1 = length // n2
    cw = groups * 2 * gd
    y = _chan_dft(f, gd)
    yt = y.reshape(n1, n2, cw).transpose(1, 0, 2).reshape(length, cw)
    c1, s1 = _dft_tables(n1)
    w1 = jnp.asarray(np.concatenate([c1, -s1], axis=0), BF16)
    ang = 2.0 * np.pi * (np.arange(n2)[:, None] * np.arange(n1)[None, :]).astype(np.float64) / length
    tc = jnp.asarray(np.cos(ang)[:, :, None], F32)
    ts = jnp.asarray(np.sin(ang)[:, :, None], F32)
    zt = pl.pallas_call(
        functools.partial(_pos_dft1_kernel, groups=groups, gd=gd),
        grid=(n2,),
        in_specs=[pl.BlockSpec((n1, cw), lambda i: (i, 0)),
                  pl.BlockSpec((2 * n1, n1), lambda i: (0, 0)),
                  pl.BlockSpec((None, n1, 1), lambda i: (i, 0, 0)),
                  pl.BlockSpec((None, n1, 1), lambda i: (i, 0, 0))],
        out_specs=pl.BlockSpec((n1, cw), lambda i: (i, 0)),
        out_shape=jax.ShapeDtypeStruct((length, cw), BF16),
        compiler_params=_params(1),
        name="fnet_pos1",
    )(yt, w1, tc, ts)
    z = zt.reshape(n2, n1, cw).transpose(1, 0, 2).reshape(length, cw)
    o = _pos_dft2(z, n2, groups, gd, 1.0 / np.sqrt(float(length) * gd))
    return o.reshape(n1, n2, width).transpose(1, 0, 2).reshape(length, width)


def _fourier_short(f, gd):
    length, width = f.shape
    groups = width // gd
    y = _chan_dft(f, gd)
    return _pos_dft2(y, length, groups, gd, 1.0 / np.sqrt(float(length) * gd))


def _merge_kernel(ya_ref, yb_ref, yc_ref, g0_ref, g1_ref, g2_ref, wa_ref, wb_ref, wc_ref, o_ref):
    acc = g0_ref[...].astype(F32) * jnp.dot(ya_ref[...], wa_ref[...], preferred_element_type=F32)
    acc = acc + g1_ref[...].astype(F32) * jnp.dot(yb_ref[...], wb_ref[...], preferred_element_type=F32)
    acc = acc + g2_ref[...].astype(F32) * jnp.dot(yc_ref[...], wc_ref[...], preferred_element_type=F32)
    o_ref[...] = acc.astype(o_ref.dtype)


def _merge(ya, yb, yc, gates, w_pa, w_pb, w_pc):
    t_all = ya.shape[0]
    d_model = w_pa.shape[1]
    tn = 1024
    tm = _pick_tile(t_all, (640, 512, 256, 128))
    nblk = d_model // tn
    return pl.pallas_call(
        _merge_kernel,
        grid=(d_model // tn, t_all // tm),
        in_specs=[pl.BlockSpec((tm, ya.shape[1]), lambda j, i: (i, 0)),
                  pl.BlockSpec((tm, yb.shape[1]), lambda j, i: (i, 0)),
                  pl.BlockSpec((tm, yc.shape[1]), lambda j, i: (i, 0)),
                  pl.BlockSpec((tm, tn), lambda j, i: (i, j)),
                  pl.BlockSpec((tm, tn), lambda j, i: (i, nblk + j)),
                  pl.BlockSpec((tm, tn), lambda j, i: (i, 2 * nblk + j)),
                  pl.BlockSpec((w_pa.shape[0], tn), lambda j, i: (0, j)),
                  pl.BlockSpec((w_pb.shape[0], tn), lambda j, i: (0, j)),
                  pl.BlockSpec((w_pc.shape[0], tn), lambda j, i: (0, j))],
        out_specs=pl.BlockSpec((tm, tn), lambda j, i: (i, j)),
        out_shape=jax.ShapeDtypeStruct((t_all, d_model), BF16),
        compiler_params=_params(2),
        name="merge",
    )(ya, yb, yc, gates, gates, gates, w_pa.astype(BF16), w_pb.astype(BF16), w_pc.astype(BF16))


def _proj_res_kernel(a_ref, w_ref, h_ref, mod_ref, o_ref, *, k_gate, t_ctx, d_model):
    tm, tn = o_ref.shape
    j = pl.program_id(0)
    is_ctx = _row_is_ctx(pl.program_id(1), tm, t_ctx)
    acc = jnp.dot(a_ref[...], w_ref[...], preferred_element_type=F32)
    gate = jnp.where(is_ctx, mod_ref[1:2, :], mod_ref[0:1, :])
    o_ref[...] = h_ref[...] + gate * acc


def _proj_res(a, w, h, mod, layer, k_gate, t_ctx):
    t_all, d_model = h.shape
    tn = 1024
    tm = _pick_tile(t_all, (640, 512, 256, 128))
    nblk = d_model // tn
    return pl.pallas_call(
        functools.partial(_proj_res_kernel, k_gate=k_gate, t_ctx=t_ctx, d_model=d_model),
        grid=(d_model // tn, t_all // tm),
        in_specs=[pl.BlockSpec((tm, a.shape[1]), lambda j, i: (i, 0)),
                  pl.BlockSpec((a.shape[1], tn), lambda j, i: (0, j)),
                  pl.BlockSpec((tm, tn), lambda j, i: (i, j)),
                  pl.BlockSpec((None, SUBLANES, tn), lambda j, i: (layer, 0, k_gate * nblk + j))],
        out_specs=pl.BlockSpec((tm, tn), lambda j, i: (i, j)),
        out_shape=jax.ShapeDtypeStruct((t_all, d_model), F32),
        compiler_params=_params(2),
        name="proj_res",
    )(a, w.astype(BF16), h, mod)


ROUTE_EID0, ROUTE_EID1, ROUTE_GATE0, ROUTE_GATE1 = 0, 1, 2, 3


def _ffn_norm_kernel(h_ref, g_ref, mod_ref, wr_ref, br_ref, x_ref, rt_ref, *, t_ctx):
    tm, d_model = h_ref.shape
    x = h_ref[...]
    y = x * lax.rsqrt(jnp.mean(x * x, axis=-1, keepdims=True) + EPS) * g_ref[...]
    is_ctx = _row_is_ctx(pl.program_id(0), tm, t_ctx)
    shift = _mod_rows(mod_ref, 3, d_model, is_ctx)
    scale = _mod_rows(mod_ref, 4, d_model, is_ctx)
    xn = y * (1.0 + scale) + shift
    n_tiles = d_model // LANES
    for j in range(n_tiles):
        x_ref[pl.ds(j, tm, stride=n_tiles), :] = xn[:, j * LANES:(j + 1) * LANES]
    logits = jnp.dot(xn, wr_ref[...], precision=HIGHEST, preferred_element_type=F32) + br_ref[...]
    lane = lax.broadcasted_iota(jnp.int32, logits.shape, 1).astype(F32)
    far = np.float32(ROUTER_COLS)
    neg = np.float32(-np.inf)
    lg = jnp.where(lane < MOE_GROUPS, logits, neg)
    m = jnp.max(lg, axis=-1, keepdims=True)
    g_top = jnp.min(jnp.where(lg == m, lane, far), axis=-1, keepdims=True)
    p_sel = 1.0 / jnp.sum(jnp.exp(lg - m), axis=-1, keepdims=True)
    lo = MOE_GROUPS + g_top * MOE_EPG
    ls = jnp.where(jnp.logical_and(lane >= lo, lane < lo + MOE_EPG), logits, neg)
    v1 = jnp.max(ls, axis=-1, keepdims=True)
    i1 = jnp.min(jnp.where(ls == v1, lane, far), axis=-1, keepdims=True)
    ls2 = jnp.where(lane == i1, neg, ls)
    v2 = jnp.max(ls2, axis=-1, keepdims=True)
    i2 = jnp.min(jnp.where(ls2 == v2, lane, far), axis=-1, keepdims=True)
    e2 = jnp.exp(v2 - v1)
    den = 1.0 / (1.0 + e2)
    rec = jnp.where(lane == ROUTE_EID0, i1 - MOE_GROUPS, 0.0)
    rec = jnp.where(lane == ROUTE_EID1, i2 - MOE_GROUPS, rec)
    rec = jnp.where(lane == ROUTE_GATE0, p_sel * den, rec)
    rec = jnp.where(lane == ROUTE_GATE1, p_sel * (e2 * den), rec)
    rt_ref[...] = rec


def _ffn_norm(h, g, mod, layer, w_router, b_router, t_ctx):
    t_all, d_model = h.shape
    tm = 256
    return pl.pallas_call(
        functools.partial(_ffn_norm_kernel, t_ctx=t_ctx),
        grid=(t_all // tm,),
        in_specs=[pl.BlockSpec((tm, d_model), lambda i: (i, 0)),
                  pl.BlockSpec((1, d_model), lambda i: (0, 0)),
                  pl.BlockSpec((None, SUBLANES, N_MOD * d_model), lambda i: (layer, 0, 0)),
                  pl.BlockSpec((d_model, ROUTER_COLS), lambda i: (0, 0)),
                  pl.BlockSpec((1, ROUTER_COLS), lambda i: (0, 0))],
        out_specs=[pl.BlockSpec((tm * (d_model // LANES), LANES), lambda i: (i, 0)),
                   pl.BlockSpec((tm, ROUTER_COLS), lambda i: (i, 0))],
        out_shape=[jax.ShapeDtypeStruct((t_all * (d_model // LANES), LANES), F32),
                   jax.ShapeDtypeStruct((t_all, ROUTER_COLS), F32)],
        compiler_params=_params(1),
        name="ffn_norm",
    )(h, g.reshape(1, d_model), mod, w_router, b_router)


def _moe_kernel(be_ref, nact_ref, first_ref, wslot_ref, nexte_ref,
                tok_ref, tok_next_ref, dst_ref, x_hbm, w1_hbm, w3_hbm, w2_hbm, bw_ref,
                y_hbm, xbuf0, xbuf1, ybuf0, ybuf1, wf1, wf3, wf2, w1b, w3b, w2b, sem_in, sem_out, sem_w,
                *, layer):
    b = pl.program_id(0)
    n_active = nact_ref[0]
    xbufs = (xbuf0, xbuf1)
    ybufs = (ybuf0, ybuf1)

    def weight_copies(e, ws):
        return [pltpu.make_async_copy(w1_hbm.at[layer, e], wf1.at[ws], sem_w.at[ws]),
                pltpu.make_async_copy(w3_hbm.at[layer, e], wf3.at[ws], sem_w.at[ws]),
                pltpu.make_async_copy(w2_hbm.at[layer, e], wf2.at[ws], sem_w.at[ws])]

    n_tiles = xbuf0.shape[0] // MOE_BLOCK

    def gather_copy(row0, r, s, sem_slot):
        return pltpu.make_async_copy(x_hbm.at[pl.ds(pl.multiple_of(row0, n_tiles), n_tiles), :],
                                     xbufs[s].at[pl.ds(r * n_tiles, n_tiles), :], sem_in.at[sem_slot])

    def scatter_copy(row0, r, s, sem_slot):
        return pltpu.make_async_copy(ybufs[s].at[pl.ds(r * n_tiles, n_tiles), :],
                                     y_hbm.at[pl.ds(pl.multiple_of(row0, n_tiles), n_tiles), :], sem_out.at[sem_slot])

    def wait_rows(copy):
        for _ in range(MOE_BLOCK):
            copy.wait()

    weight_queue, row_queue = 1, 0

    @pl.when(b == 0)
    def _():
        for cp in weight_copies(be_ref[0], 0):
            cp.start(priority=weight_queue)
        for r in range(MOE_BLOCK):
            gather_copy(tok_ref[0, r], r, 0, 0).start(priority=row_queue)

    def step(s):
        @pl.when(first_ref[b] == 1)
        def _():
            ws = wslot_ref[b]
            for cp in weight_copies(be_ref[b], ws):
                cp.wait()
            w1b[...] = wf1[ws].astype(BF16)
            w3b[...] = wf3[ws].astype(BF16)
            w2b[...] = wf2[ws].astype(BF16)
            nxt = nexte_ref[b]

            @pl.when(nxt >= 0)
            def _():
                for cp in weight_copies(nxt, 1 - ws):
                    cp.start(priority=weight_queue)

        wait_rows(gather_copy(0, 0, s, s))

        @pl.when(b >= 2)
        def _():
            wait_rows(scatter_copy(0, 0, s, s))

        for r in range(MOE_BLOCK):
            gather_copy(tok_next_ref[0, r], r, 1 - s, 1 - s).start(priority=row_queue)
        x = jnp.concatenate([xbufs[s][pl.ds(j, MOE_BLOCK, stride=n_tiles), :].astype(BF16)
                             for j in range(n_tiles)], axis=1)
        h1 = jnp.dot(x, w1b[...], preferred_element_type=F32)
        h3 = jnp.dot(x, w3b[...], preferred_element_type=F32)
        hh = (_silu(h1) * h3).astype(BF16)
        y = jnp.dot(hh, w2b[...], preferred_element_type=F32) * bw_ref[...]
        for j in range(n_tiles):
            ybufs[s][pl.ds(j, MOE_BLOCK, stride=n_tiles), :] = y[:, j * LANES:(j + 1) * LANES]
        for r in range(MOE_BLOCK):
            scatter_copy(dst_ref[0, r], r, s, s).start(priority=r % 2)

    active = b < n_active
    for s in range(2):
        pl.when(jnp.logical_and(active, b % 2 == s))(functools.partial(step, s))

    @pl.when(b == n_active - 1)
    def _():
        sb = b % 2
        wait_rows(gather_copy(0, 0, 0, 1 - sb))

        @pl.when(b >= 1)
        def _():
            wait_rows(scatter_copy(0, 0, 0, 1 - sb))
        wait_rows(scatter_copy(0, 0, 0, sb))
        ybuf0[...] = jnp.zeros(ybuf0.shape, F32)
        rows = MOE_BLOCK * n_tiles
        spare0 = y_hbm.shape[0] - 2 * rows
        fills = [pltpu.make_async_copy(ybuf0, y_hbm.at[pl.ds(spare0 + k * rows, rows), :], sem_out.at[0])
                 for k in range(2)]
        for cp in fills:
            cp.start()
        for cp in fills:
            cp.wait()


def _moe_blocks(layer, tables, xn, w1, w3, w2, n_out_rows):
    block_e, n_active, first, wslot, next_e, tok_idx, dst_idx, buf_w = tables
    d_model, ff = w1.shape[2], w1.shape[3]
    n_tiles = d_model // LANES
    n_blocks = tok_idx.shape[0]
    slab = (MOE_BLOCK * n_tiles, LANES)
    idx_spec = functools.partial(pl.BlockSpec, (None, 1, MOE_BLOCK), memory_space=pltpu.SMEM)
    grid_spec = pltpu.PrefetchScalarGridSpec(
        num_scalar_prefetch=5,
        grid=(n_blocks,),
        in_specs=[idx_spec(index_map=lambda b, *_: (b, 0, 0)),
                  idx_spec(index_map=lambda b, *_: (jnp.minimum(b + 1, n_blocks - 1), 0, 0)),
                  idx_spec(index_map=lambda b, *_: (b, 0, 0)),
                  pl.BlockSpec(memory_space=pl.ANY),
                  pl.BlockSpec(memory_space=pl.ANY),
                  pl.BlockSpec(memory_space=pl.ANY),
                  pl.BlockSpec(memory_space=pl.ANY),
                  pl.BlockSpec((MOE_BLOCK, 1), lambda b, *_: (b, 0))],
        out_specs=pl.BlockSpec(memory_space=pl.ANY),
        scratch_shapes=[pltpu.VMEM(slab, F32), pltpu.VMEM(slab, F32), pltpu.VMEM(slab, F32), pltpu.VMEM(slab, F32),
                        pltpu.VMEM((2, d_model, ff), F32), pltpu.VMEM((2, d_model, ff), F32),
                        pltpu.VMEM((2, ff, d_model), F32),
                        pltpu.VMEM((d_model, ff), BF16), pltpu.VMEM((d_model, ff), BF16),
                        pltpu.VMEM((ff, d_model), BF16),
                        pltpu.SemaphoreType.DMA((2,)), pltpu.SemaphoreType.DMA((2,)),
                        pltpu.SemaphoreType.DMA((2,))],
    )
    return pl.pallas_call(
        functools.partial(_moe_kernel, layer=layer),
        grid_spec=grid_spec,
        out_shape=jax.ShapeDtypeStruct((n_out_rows * n_tiles, LANES), F32),
        compiler_params=_params(1),
        name="moe_blocks",
    )(block_e, n_active, first, wslot, next_e, tok_idx, tok_idx, dst_idx, xn, w1, w3, w2, buf_w.reshape(-1, 1))


def _dispatch(route, t_all, n_tiles):
    eid = route[:, ROUTE_EID0:ROUTE_EID1 + 1].astype(jnp.int32)
    gate = route[:, ROUTE_GATE0:ROUTE_GATE1 + 1]
    n_asg = t_all * MOE_TOPK
    flat_e = eid.reshape(n_asg)
    flat_w = gate.reshape(n_asg)
    _, sorted_a, sorted_w = lax.sort((flat_e, jnp.arange(n_asg, dtype=jnp.int32), flat_w), num_keys=1)
    experts = jnp.arange(MOE_EXPERTS, dtype=jnp.int32)
    counts = jnp.sum((flat_e[:, None] == experts[None, :]).astype(jnp.int32), axis=0)
    padded = (counts + MOE_BLOCK - 1) // MOE_BLOCK * MOE_BLOCK
    pad_end = jnp.cumsum(padded)
    pad_start = pad_end - padded
    start = jnp.cumsum(counts) - counts
    n_blocks = (n_asg + MOE_EXPERTS * (MOE_BLOCK - 1)) // MOE_BLOCK + 1
    blk = jnp.arange(n_blocks, dtype=jnp.int32)
    blk_e = jnp.minimum(jnp.sum((pad_end[None, :] <= (blk * MOE_BLOCK)[:, None]).astype(jnp.int32), axis=1),
                        MOE_EXPERTS - 1)
    n_active = pad_end[-1] // MOE_BLOCK
    is_act = blk < n_active
    block_e = jnp.where(is_act, blk_e, 0)
    prev_e = jnp.concatenate([jnp.full((1,), -1, jnp.int32), blk_e[:-1]])
    first = jnp.logical_and(is_act, blk_e != prev_e).astype(jnp.int32)
    has_tok = counts > 0
    rank = jnp.cumsum(has_tok.astype(jnp.int32)) - 1
    later = lax.cummin(jnp.where(has_tok, experts, MOE_EXPERTS)[::-1])[::-1]
    nxt = jnp.concatenate([later[1:], jnp.full((1,), MOE_EXPERTS, jnp.int32)])
    per_expert = jnp.stack([pad_start, counts, start, rank, jnp.where(nxt >= MOE_EXPERTS, -1, nxt)], axis=1)
    onehot = blk_e[:, None] == experts[None, :]
    per_blk = jnp.sum(jnp.where(onehot[:, :, None], per_expert[None, :, :], 0), axis=1)
    pad_start_b, counts_b, start_b, rank_b, next_e = (per_blk[:, k] for k in range(5))
    wslot = rank_b % 2
    r_in_blk = jnp.arange(MOE_BLOCK, dtype=jnp.int32)[None, :]
    r = (blk * MOE_BLOCK - pad_start_b)[:, None] + r_in_blk
    valid = jnp.logical_and(r < counts_b[:, None], is_act[:, None])
    p = jnp.clip(start_b[:, None] + r, 0, n_asg - 1)
    packed = jnp.stack([sorted_a, lax.bitcast_convert_type(sorted_w, jnp.int32)], axis=1)
    got = jnp.take(packed, p, axis=0, mode="clip")
    a = got[..., 0]
    tok = jnp.where(valid, a >> 1, 0)
    spare = MOE_TOPK * t_all + (blk % 2)[:, None] * MOE_BLOCK + r_in_blk
    dst = jnp.where(valid, (a & 1) * t_all + (a >> 1), spare)
    w = jnp.where(valid, lax.bitcast_convert_type(got[..., 1], F32), 0.0)
    return (block_e.astype(jnp.int32), n_active.astype(jnp.int32).reshape(1),
            first, wslot.astype(jnp.int32), next_e.astype(jnp.int32),
            (tok * n_tiles).reshape(n_blocks, 1, MOE_BLOCK).astype(jnp.int32),
            (dst * n_tiles).reshape(n_blocks, 1, MOE_BLOCK).astype(jnp.int32), w.reshape(-1))


def _moe_res_norm_kernel(h_ref, y0_ref, y1_ref, mod_ref, g_ref, *rest, t_ctx, tile0, last):
    tm, d_model = h_ref.shape
    n_tiles = d_model // LANES
    is_ctx = _row_is_ctx(pl.program_id(0) + tile0, tm, t_ctx)
    ysum = jnp.concatenate([y0_ref[pl.ds(j, tm, stride=n_tiles), :] + y1_ref[pl.ds(j, tm, stride=n_tiles), :]
                            for j in range(n_tiles)], axis=1)
    h = h_ref[...] + _mod_rows(mod_ref, 5, d_model, is_ctx) * ysum
    y = h * lax.rsqrt(jnp.mean(h * h, axis=-1, keepdims=True) + EPS) * g_ref[...]
    if last:
        (o_ref,) = rest
        o_ref[...] = y
    else:
        nmod_ref, ho_ref, hn_ref = rest
        ho_ref[...] = h
        shift = _mod_rows(nmod_ref, 0, d_model, is_ctx)
        scale = _mod_rows(nmod_ref, 1, d_model, is_ctx)
        hn_ref[...] = (y * (1.0 + scale) + shift).astype(hn_ref.dtype)


def _moe_res_norm(h, y01, mod, layer, g_next, t_ctx, last):
    t_all, d_model = h.shape
    tm = 256
    nt = t_all // tm
    tile0 = t_ctx // tm if last else 0
    row = pl.BlockSpec((tm, d_model), lambda i: (i + tile0, 0))
    yrows = (tm * (d_model // LANES), LANES)
    mod_spec = lambda lyr: pl.BlockSpec((None, SUBLANES, N_MOD * d_model), lambda i: (lyr, 0, 0))
    in_specs = [row, pl.BlockSpec(yrows, lambda i: (i + tile0, 0)),
                pl.BlockSpec(yrows, lambda i: (i + tile0 + nt, 0)), mod_spec(layer),
                pl.BlockSpec((1, d_model), lambda i: (0, 0))]
    args = [h, y01, y01, mod, g_next.reshape(1, d_model)]
    out_row = pl.BlockSpec((tm, d_model), lambda i: (i, 0))
    if last:
        out_specs = out_row
        out_shape = jax.ShapeDtypeStruct((t_all - t_ctx, d_model), F32)
    else:
        in_specs.append(mod_spec(layer + 1))
        args.append(mod)
        out_specs = [out_row, out_row]
        out_shape = [jax.ShapeDtypeStruct((t_all, d_model), F32), jax.ShapeDtypeStruct((t_all, d_model), BF16)]
    return pl.pallas_call(
        functools.partial(_moe_res_norm_kernel, t_ctx=t_ctx, tile0=tile0, last=last),
        grid=(nt - tile0,),
        in_specs=in_specs,
        out_specs=out_specs,
        out_shape=out_shape,
        compiler_params=_params(1),
        name="moe_res_norm",
    )(*args)


def kernel(x, c, ctx, c_ctx, w_ada, b_ada, g_mix, w_in, b_gate, ln_a_g, ln_a_b, w_sp, b_sp, conv_w, conv_b, dt_bias, a_log, d_skip, g_ssm, w_pa, w_pb, w_pc, w_out, g_ffn, w_rg, b_rg, w_re, b_re, w_e1, w_e3, w_e2, g_final):
    bsz, t_lat, d_model = x.shape
    assert bsz == 1 and c.shape[0] == 1
    t_ctx = ctx.shape[1]
    depth = w_ada.shape[0]
    a_width = ln_a_g.shape[1]
    inner = SSM_HEADS * SSM_HEAD_DIM
    conv_dim = conv_w.shape[2]
    f_width = w_pc.shape[1]
    f_gd = f_width // F_GROUPS
    n_ctx_chunks = t_ctx // CHUNK
    t_all = t_ctx + t_lat
    assert t_ctx % 256 == 0 and t_lat % 256 == 0 and t_lat % (CHUNK * SUBLANES) == 0

    c_u, c_z = 0, 2 * a_width
    c_xbc = c_z + inner
    c_dt = c_xbc + conv_dim
    c_fc = c_dt + 2 * SSM_HEADS
    c_gl = c_fc + f_width
    n_gl = N_BRANCH * d_model
    p_z, p_xbc = 2 * a_width, 2 * a_width + inner

    cs = jnp.zeros((SUBLANES, d_model), F32).at[0].set(c[0]).at[1].set(c_ctx)
    mod = _ada(cs, w_ada, b_ada)

    h = jnp.concatenate([ctx[0], x[0]], axis=0)
    pad_heads = jnp.zeros((128 - 2 * SSM_HEADS,), F32)

    for l in range(depth):
        wl = w_in[l]
        w_main = wl[:, c_u:c_dt].astype(BF16)
        w_gl = wl[:, c_gl:c_gl + n_gl].astype(BF16)
        w_dt = jnp.concatenate([wl[:, c_dt:c_fc], jnp.zeros((d_model, 128 - 2 * SSM_HEADS), F32)], axis=1).astype(BF16)
        w_fc = wl[:, c_fc:c_gl].astype(BF16)

        if l == 0:
            hn = _norm_mod(h, g_mix[l], mod, l, 0, 1, t_ctx, BF16)
        proj = _mm(hn, w_main, F32, 1024, "proj_main")
        gates = _mm_gate(hn, w_gl, b_gate[l], 1024)
        dt_raw = _mm(hn, w_dt, F32, 128, "proj_dt")
        fc = _mm(hn, w_fc, BF16, 1024, "proj_fc")

        ya = _sgu(proj, ln_a_g[l], ln_a_b[l], w_sp[l], b_sp[l], a_width)

        dtb = jnp.concatenate([dt_bias[l].reshape(-1), pad_heads]).reshape(1, 128)
        alg = jnp.concatenate([a_log[l].reshape(-1), pad_heads]).reshape(1, 128)
        dsk = jnp.repeat(d_skip[l], SSM_HEAD_DIM, axis=-1)
        assert p_xbc % conv_dim == 0 and p_z % inner == 0
        y_f = _ssd(proj, p_xbc // conv_dim, dt_raw, conv_w[l], conv_b[l], dtb, alg, dsk[0:1], 0, n_ctx_chunks)
        y_b = _ssd(proj, p_xbc // conv_dim, dt_raw, conv_w[l], conv_b[l], dtb, alg, dsk[1:2], 1, n_ctx_chunks)
        yb = _gate_norm(y_f, y_b, proj, p_z // inner, g_ssm[l])

        yc_lat = _fourier_long(fc[t_ctx:], f_gd)
        if l < depth - 1:
            yc_ctx = _fourier_short(fc[:t_ctx], f_gd)
        else:
            yc_ctx = jnp.zeros((t_ctx, f_width), BF16)
        yc = jnp.concatenate([yc_ctx, yc_lat], axis=0)

        merged = _merge(ya, yb, yc, gates, w_pa[l], w_pb[l], w_pc[l])
        h = _proj_res(merged, w_out[l], h, mod, l, 2, t_ctx)

        w_router = jnp.concatenate([w_rg[l], w_re[l], jnp.zeros((d_model, ROUTER_COLS - MOE_GROUPS - MOE_EXPERTS), F32)], axis=1)
        b_router = jnp.concatenate([b_rg[l], b_re[l], jnp.zeros((ROUTER_COLS - MOE_GROUPS - MOE_EXPERTS,), F32)]).reshape(1, ROUTER_COLS)
        xn, route = _ffn_norm(h, g_ffn[l], mod, l, w_router, b_router, t_ctx)
        y01 = _moe_blocks(l, _dispatch(route, t_all, d_model // LANES), xn, w_e1, w_e3, w_e2,
                          MOE_TOPK * t_all + 2 * MOE_BLOCK)
        if l < depth - 1:
            h, hn = _moe_res_norm(h, y01, mod, l, g_mix[l + 1], t_ctx, False)
        else:
            out = _moe_res_norm(h, y01, mod, l, g_final, t_ctx, True)
    return out.reshape(bsz, t_lat, d_model)
```
